```python
import jax, jax.numpy as jnp
from jax import lax
import numpy as np

D_MODEL = 1024
BATCH = 8
SEQ = 4096
DEPTH = 1
DEC_BATCH = 32
DEC_SEQ = 32
PAST_LEN = 2048

CHUNK = 64
N_PAST_CHUNKS = 8
N_BAND = N_PAST_CHUNKS + 1
D_CONV = D_MODEL // 2
D_ATT = D_MODEL - D_CONV
HEAD_DIM = 64
N_HEADS = D_ATT // HEAD_DIM
CONV_W = 3
REL_CLIP = 128
N_REL = 2 * REL_CLIP + 1
D_PLE = 256
EPS = 1e-6
SPLITS = (D_CONV, D_CONV, D_CONV, D_CONV, D_ATT, D_ATT, D_ATT, D_ATT)
D_IN = sum(SPLITS)

kernel_name = "hybrid_conv_chunkattn_streaming_step"


def rmsnorm(x, g):
    xf = x.astype(jnp.float32)
    r = lax.rsqrt(jnp.mean(xf * xf, axis=-1, keepdims=True) + EPS)
    return (xf * r * g.astype(jnp.float32)).astype(x.dtype)


def project_in(xn, w_in):
    z = xn @ w_in
    offsets = [int(o) for o in np.cumsum(SPLITS)[:-1]]
    return jnp.split(z, offsets, axis=-1)


def short_conv(u_ext, w, length):
    out = w[0] * u_ext[:, 0:length]
    for t in range(1, CONV_W):
        out = out + w[t] * u_ext[:, t:t + length]
    return out


def band_attention(q, k, v, rel, valid, rel_bias):
    bias = rel_bias[:, jnp.clip(rel, -REL_CLIP, REL_CLIP) + REL_CLIP]
    s = jnp.einsum('...qhd,...khd->...hqk', q, k).astype(jnp.float32) * (HEAD_DIM ** -0.5)
    s = s + bias.astype(jnp.float32)
    if valid is not None:
        s = jnp.where(valid, s, -1e30)
    pr = jax.nn.softmax(s, axis=-1).astype(v.dtype)
    return jnp.einsum('...hqk,...khd->...qhd', pr, v)


def merge_out(yc, ya, norm_conv, norm_att, w_out):
    y = jnp.concatenate([rmsnorm(yc, norm_conv), rmsnorm(ya, norm_att)], axis=-1)
    return y @ w_out


def per_layer_embed(x, p, ple_norm, w_ple_gate, w_ple_proj):
    gate = jax.nn.sigmoid(rmsnorm(x, ple_norm) @ w_ple_gate)
    return x + gate * (p @ w_ple_proj)


def prompt_layer(x, p, norm_in, w_in, conv_w, rel_bias, norm_conv, norm_att, w_out,
                 ple_norm, w_ple_gate, w_ple_proj):
    B, S, _ = x.shape
    xn = rmsnorm(x, norm_in)
    h, bg, cg, zc, q, k, v, za = project_in(xn, w_in)
    u = cg * h
    u_ext = jnp.pad(u, ((0, 0), (CONV_W - 1, 0), (0, 0)))
    yc = bg * short_conv(u_ext, conv_w, S) * jax.nn.silu(zc)
    conv_state = u_ext[:, S:]
    nc = S // CHUNK
    qc = q.reshape(B, nc, CHUNK, N_HEADS, HEAD_DIM)
    pad = ((0, 0), (N_PAST_CHUNKS, 0), (0, 0), (0, 0), (0, 0))
    kp = jnp.pad(k.reshape(B, nc, CHUNK, N_HEADS, HEAD_DIM), pad)
    vp = jnp.pad(v.reshape(B, nc, CHUNK, N_HEADS, HEAD_DIM), pad)
    k_band = jnp.concatenate([kp[:, o:o + nc] for o in range(N_BAND)], axis=2)
    v_band = jnp.concatenate([vp[:, o:o + nc] for o in range(N_BAND)], axis=2)
    chunk_id = jnp.arange(nc)[:, None] - N_PAST_CHUNKS + jnp.arange(N_BAND)[None, :]
    valid = jnp.repeat(chunk_id >= 0, CHUNK, axis=1)[None, :, None, None, :]
    slot = jnp.arange(N_BAND * CHUNK)
    rel = (N_BAND - 1) * CHUNK + jnp.arange(CHUNK)[:, None] - slot[None, :]
    ya = band_attention(qc, k_band, v_band, rel, valid, rel_bias).reshape(B, S, D_ATT)
    ya = ya * jax.nn.silu(za)
    win = min(N_PAST_CHUNKS * CHUNK, S)
    k_state = k.reshape(B, S, N_HEADS, HEAD_DIM)[:, S - win:]
    v_state = v.reshape(B, S, N_HEADS, HEAD_DIM)[:, S - win:]
    x = x + merge_out(yc, ya, norm_conv, norm_att, w_out)
    x = per_layer_embed(x, p, ple_norm, w_ple_gate, w_ple_proj)
    return x, k_state, v_state, conv_state


def sample_layer(x, p, cache_k, cache_v, conv_buf, norm_in, w_in, conv_w, rel_bias, norm_conv,
                 norm_att, w_out, ple_norm, w_ple_gate, w_ple_proj):
    B, S, _ = x.shape
    xn = rmsnorm(x, norm_in)
    h, bg, cg, zc, q, k, v, za = project_in(xn, w_in)
    u = cg * h
    u_ext = jnp.concatenate([conv_buf.astype(u.dtype), u], axis=1)
    yc = bg * short_conv(u_ext, conv_w, S) * jax.nn.silu(zc)
    conv_state = u_ext[:, S:]
    qh = q.reshape(B, S, N_HEADS, HEAD_DIM)
    kh = k.reshape(B, S, N_HEADS, HEAD_DIM)
    vh = v.reshape(B, S, N_HEADS, HEAD_DIM)
    kv_win = cache_k.shape[1]
    k_all = jnp.concatenate([cache_k.astype(kh.dtype), kh], axis=1)
    v_all = jnp.concatenate([cache_v.astype(vh.dtype), vh], axis=1)
    kpos = jnp.concatenate([jnp.arange(kv_win) - kv_win, jnp.arange(S)])
    rel = jnp.arange(S)[:, None] - kpos[None, :]
    ya = band_attention(qh, k_all, v_all, rel, None, rel_bias).reshape(B, S, D_ATT)
    ya = ya * jax.nn.silu(za)
    x = x + merge_out(yc, ya, norm_conv, norm_att, w_out)
    x = per_layer_embed(x, p, ple_norm, w_ple_gate, w_ple_proj)
    return x, kh, vh, conv_state


def setup_inputs(seed: int = 0) -> dict:
    key = jax.random.key(seed)
    ks = jax.random.split(key, 20)
    kv_win = min(N_PAST_CHUNKS * CHUNK, PAST_LEN)
    f32 = jnp.float32
    nrm = lambda k, shape, s: jax.random.normal(k, shape, f32) * s
    return {
        "x_prompt": nrm(ks[0], (BATCH, SEQ, D_MODEL), 1.0),
        "x_sample": nrm(ks[1], (DEC_BATCH, DEC_SEQ, D_MODEL), 1.0),
        "cache_k": nrm(ks[2], (DEPTH, DEC_BATCH, kv_win, N_HEADS, HEAD_DIM), 1.0),
        "cache_v": nrm(ks[3], (DEPTH, DEC_BATCH, kv_win, N_HEADS, HEAD_DIM), 1.0),
        "state_conv": nrm(ks[4], (DEPTH, DEC_BATCH, CONV_W - 1, D_CONV), 1.0),
        "p_prompt": nrm(ks[5], (DEPTH, BATCH, SEQ, D_PLE), 1.0),
        "p_sample": nrm(ks[6], (DEPTH, DEC_BATCH, DEC_SEQ, D_PLE), 1.0),
        "norm_in": 1.0 + nrm(ks[7], (DEPTH, D_MODEL), 0.02),
        "w_in": nrm(ks[8], (DEPTH, D_MODEL, D_IN), D_MODEL ** -0.5),
        "conv_w": nrm(ks[9], (DEPTH, CONV_W, D_CONV), CONV_W ** -0.5),
        "rel_bias": nrm(ks[10], (DEPTH, N_HEADS, N_REL), 0.5),
        "norm_conv": 1.0 + nrm(ks[11], (DEPTH, D_CONV), 0.02),
        "norm_att": 1.0 + nrm(ks[12], (DEPTH, D_ATT), 0.02),
        "w_out": nrm(ks[13], (DEPTH, D_CONV + D_ATT, D_MODEL), (D_CONV + D_ATT) ** -0.5),
        "ple_norm": 1.0 + nrm(ks[14], (DEPTH, D_MODEL), 0.02),
        "w_ple_gate": nrm(ks[15], (DEPTH, D_MODEL, D_MODEL), D_MODEL ** -0.5),
        "w_ple_proj": nrm(ks[16], (DEPTH, D_PLE, D_MODEL), D_PLE ** -0.5),
        "final_norm": 1.0 + nrm(ks[17], (D_MODEL,), 0.02),
    }


def reference(x_prompt, x_sample, cache_k, cache_v, state_conv, p_prompt, p_sample, norm_in, w_in,
              conv_w, rel_bias, norm_conv, norm_att, w_out, ple_norm, w_ple_gate, w_ple_proj,
              final_norm):
    xp, xs = x_prompt, x_sample
    kp_l, vp_l, cp_l, ks_l, vs_l, cs_l = [], [], [], [], [], []
    for i in range(DEPTH):
        xp, kpi, vpi, cpi = prompt_layer(xp, p_prompt[i], norm_in[i], w_in[i], conv_w[i], rel_bias[i],
                                         norm_conv[i], norm_att[i], w_out[i], ple_norm[i],
                                         w_ple_gate[i], w_ple_proj[i])
        xs, ksi, vsi, csi = sample_layer(xs, p_sample[i], cache_k[i], cache_v[i], state_conv[i],
                                         norm_in[i], w_in[i], conv_w[i], rel_bias[i], norm_conv[i],
                                         norm_att[i], w_out[i], ple_norm[i], w_ple_gate[i],
                                         w_ple_proj[i])
        kp_l.append(kpi); vp_l.append(vpi); cp_l.append(cpi)
        ks_l.append(ksi); vs_l.append(vsi); cs_l.append(csi)
    y_prompt = rmsnorm(xp, final_norm)
    y_sample = rmsnorm(xs, final_norm)
    k_prompt = jnp.stack(kp_l)
    v_prompt = jnp.stack(vp_l)
    conv_prompt = jnp.stack(cp_l)
    k_sample = jnp.stack(ks_l)
    v_sample = jnp.stack(vs_l)
    conv_sample = jnp.stack(cs_l)
    return (y_prompt, y_sample, k_prompt, v_prompt, conv_prompt, k_sample, v_sample, conv_sample)
```

```python
import functools

import jax
import jax.numpy as jnp
from jax import lax
from jax.experimental import pallas as pl
from jax.experimental.pallas import tpu as pltpu

D_MODEL = 1024
D_CONV = 512
D_ATT = 512
HEAD_DIM = 64
N_HEADS = 8
N_PAIRS = N_HEADS // 2
PAIR_W = 2 * HEAD_DIM
CHUNK = 64
N_PAST_CHUNKS = 8
HIST = N_PAST_CHUNKS * CHUNK
REL_CLIP = 128
D_PLE = 256
CONV_W = 3
EPS = 1e-6
NEG = -1e30
SCALE = HEAD_DIM ** -0.5

GROUP = 2 * CHUNK
KWIN = HIST + GROUP
PROMPT_TILE = 256
SAMPLE_BATCH_PER_STEP = 4
VMEM_LIMIT_BYTES = 56 * 1024 * 1024

F32 = jnp.float32
BF16 = jnp.bfloat16


def _rms(x, g):
    ms = jnp.mean(x * x, axis=-1, keepdims=True)
    return x * lax.rsqrt(ms + EPS) * g


def _sigmoid(x):
    return 1.0 / (1.0 + jnp.exp(-x))


def _dot(a, b):
    return jnp.dot(a, b, preferred_element_type=F32)


def _head(x, win_ref, nin_ref, cw_ref, ncv_ref, prev_rows):
    xn = _rms(x, nin_ref[...]).astype(BF16)

    def proj(i):
        return _dot(xn, win_ref[:, 512 * i:512 * (i + 1)])

    u = proj(2) * proj(0)
    u1, u2 = prev_rows(u)
    cv = cw_ref[0:1, :] * u2 + cw_ref[1:2, :] * u1 + cw_ref[2:3, :] * u
    zc = proj(3)
    yc = proj(1) * cv * (zc * _sigmoid(zc))
    yc_n = _rms(yc, ncv_ref[...])
    q = (proj(4) * SCALE).astype(BF16)
    return u, yc_n, q, proj(5), proj(6), proj(7)


def _attend(q2, kw, vw, bias, colmask):
    r = q2.shape[0]
    lane = lax.broadcasted_iota(jnp.int32, q2.shape, 1)
    zero = jnp.zeros_like(q2)
    qs = jnp.concatenate([jnp.where(lane < HEAD_DIM, q2, zero),
                          jnp.where(lane >= HEAD_DIM, q2, zero)], axis=0)
    s = lax.dot_general(qs, kw, (((1,), (1,)), ((), ())), preferred_element_type=F32)
    s = s + bias
    if colmask is not None:
        s = jnp.where(colmask, s, NEG)
    m = jnp.max(s, axis=-1, keepdims=True)
    e = jnp.exp(s - m)
    l = jnp.sum(e, axis=-1, keepdims=True)
    o = _dot(e.astype(BF16), vw) * (1.0 / l)
    lane_o = lax.broadcasted_iota(jnp.int32, (r, PAIR_W), 1)
    return jnp.where(lane_o < HEAD_DIM, o[:r], o[r:])


def _tail(x, yc_n, ya_n, p, wout_ref, wg_ref, wp_ref, npl_ref, nfin_ref):
    y = _dot(yc_n.astype(BF16), wout_ref[0:D_CONV, :]) + _dot(ya_n.astype(BF16), wout_ref[D_CONV:, :])
    x1 = x + y
    gate = _sigmoid(_dot(_rms(x1, npl_ref[...]).astype(BF16), wg_ref[...]))
    x2 = x1 + gate * _dot(p.astype(BF16), wp_ref[...])
    return _rms(x2, nfin_ref[...])


def _prompt_kernel(x_ref, p_ref, win_ref, wout_ref, wg_ref, wp_ref, nin_ref, cw_ref, ncv_ref,
                   nat_ref, npl_ref, nfin_ref, bias_ref,
                   y_ref, ks_ref, vs_ref, cs_ref, kbuf, vbuf, ucar):
    t = pl.program_id(1)
    n_t = pl.num_programs(1)
    tile = x_ref.shape[0]

    @pl.when(t == 0)
    def _():
        kbuf[0:HIST, :] = jnp.zeros((HIST, D_ATT), BF16)
        vbuf[0:HIST, :] = jnp.zeros((HIST, D_ATT), BF16)
        ucar[...] = jnp.zeros(ucar.shape, F32)

    x = x_ref[...]
    row = lax.broadcasted_iota(jnp.int32, (tile, D_CONV), 0)

    def prev_rows(u):
        c1 = ucar[7:8, :]
        c2 = ucar[6:7, :]
        u1 = jnp.where(row == 0, c1, pltpu.roll(u, 1, 0))
        u2 = jnp.where(row == 0, c2, jnp.where(row == 1, c1, pltpu.roll(u, 2, 0)))
        return u1, u2

    u, yc_n, q, k, v, za = _head(x, win_ref, nin_ref, cw_ref, ncv_ref, prev_rows)
    ucar[...] = u[tile - 8:, :]
    kbuf[HIST:, :] = k.astype(BF16)
    vbuf[HIST:, :] = v.astype(BF16)

    @pl.when(t >= n_t - HIST // tile)
    def _():
        ks_ref[...] = k
        vs_ref[...] = v

    @pl.when(t == n_t - 1)
    def _():
        cs_ref[...] = u[tile - 8:, :]

    col = lax.broadcasted_iota(jnp.int32, (2 * GROUP, KWIN), 1)
    groups = []
    for g in range(tile // GROUP):
        colmask = col >= HIST - t * tile - g * GROUP
        pairs = []
        for p in range(N_PAIRS):
            lanes = slice(p * PAIR_W, (p + 1) * PAIR_W)
            pairs.append(_attend(q[g * GROUP:(g + 1) * GROUP, lanes],
                                 kbuf[g * GROUP:g * GROUP + KWIN, lanes],
                                 vbuf[g * GROUP:g * GROUP + KWIN, lanes],
                                 bias_ref[p], colmask))
        groups.append(jnp.concatenate(pairs, axis=1))
    ya = jnp.concatenate(groups, axis=0) * (za * _sigmoid(za))
    ya_n = _rms(ya, nat_ref[...])

    y_ref[...] = _tail(x, yc_n, ya_n, p_ref[...], wout_ref, wg_ref, wp_ref, npl_ref, nfin_ref)

    k_keep = kbuf[tile:, :]
    v_keep = vbuf[tile:, :]
    kbuf[0:HIST, :] = k_keep
    vbuf[0:HIST, :] = v_keep


def _sample_kernel(x_ref, p_ref, ck_ref, cv_ref, sc_ref, win_ref, wout_ref, wg_ref, wp_ref, nin_ref,
                   cw_ref, ncv_ref, nat_ref, npl_ref, nfin_ref, bias_ref,
                   y_ref, ks_ref, vs_ref, cs_ref, kbuf, vbuf):
    nb = ck_ref.shape[0]
    rows = x_ref.shape[0]
    seq = rows // nb

    @pl.when(pl.program_id(0) == 0)
    def _():
        kbuf[HIST + seq:, :] = jnp.zeros((KWIN - HIST - seq, D_ATT), BF16)
        vbuf[HIST + seq:, :] = jnp.zeros((KWIN - HIST - seq, D_ATT), BF16)

    x = x_ref[...]
    rmod = lax.broadcasted_iota(jnp.int32, (rows, D_CONV), 0) % seq

    def prev_rows(u):
        e1 = jnp.concatenate([jnp.broadcast_to(sc_ref[j, 1:2, :], (seq, D_CONV)) for j in range(nb)], axis=0)
        e2 = jnp.concatenate([jnp.broadcast_to(sc_ref[j, 0:1, :], (seq, D_CONV)) for j in range(nb)], axis=0)
        u1 = jnp.where(rmod == 0, e1, pltpu.roll(u, 1, 0))
        u2 = jnp.where(rmod == 0, e2, jnp.where(rmod == 1, e1, pltpu.roll(u, 2, 0)))
        return u1, u2

    u, yc_n, q, k, v, za = _head(x, win_ref, nin_ref, cw_ref, ncv_ref, prev_rows)
    ks_ref[...] = k
    vs_ref[...] = v
    kb = k.astype(BF16)
    vb = v.astype(BF16)

    batches = []
    for j in range(nb):
        rs = slice(j * seq, (j + 1) * seq)
        cs_ref[j] = u[(j + 1) * seq - 8:(j + 1) * seq, :]
        kbuf[0:HIST, :] = ck_ref[j].astype(BF16)
        vbuf[0:HIST, :] = cv_ref[j].astype(BF16)
        kbuf[HIST:HIST + seq, :] = kb[rs, :]
        vbuf[HIST:HIST + seq, :] = vb[rs, :]
        pairs = []
        for p in range(N_PAIRS):
            lanes = slice(p * PAIR_W, (p + 1) * PAIR_W)
            pairs.append(_attend(q[rs, lanes], kbuf[:, lanes], vbuf[:, lanes], bias_ref[p], None))
        batches.append(jnp.concatenate(pairs, axis=1))
    ya = jnp.concatenate(batches, axis=0) * (za * _sigmoid(za))
    ya_n = _rms(ya, nat_ref[...])

    y_ref[...] = _tail(x, yc_n, ya_n, p_ref[...], wout_ref, wg_ref, wp_ref, npl_ref, nfin_ref)


def _bias_tables(rel_bias, seq_sample):
    s = jnp.arange(KWIN)[None, :]

    def table(n_rows, in_band):
        i = jnp.arange(n_rows)[:, None]
        idx = jnp.clip(HIST + i - s, -REL_CLIP, REL_CLIP) + REL_CLIP
        tb = jnp.where(in_band(i, s)[None], rel_bias[:, idx], NEG)
        return tb.reshape(N_PAIRS, 2 * n_rows, KWIN).astype(F32)

    prompt = table(GROUP, lambda i, s: jnp.where(i < CHUNK, s < HIST + CHUNK, s >= CHUNK))
    sample = table(seq_sample, lambda i, s: jnp.broadcast_to(s < HIST + seq_sample, (seq_sample, KWIN)))
    return prompt, sample


def _const_spec(shape):
    nd = len(shape)
    return pl.BlockSpec(shape, lambda *_: (0,) * nd, pipeline_mode=pl.Buffered(1))


def kernel(x_prompt, x_sample, cache_k, cache_v, state_conv, p_prompt, p_sample, norm_in, w_in, conv_w,
           rel_bias, norm_conv, norm_att, w_out, ple_norm, w_ple_gate, w_ple_proj, final_norm):
    depth = w_in.shape[0]
    assert depth == 1, "single-layer kernel"
    batch, seq, _ = x_prompt.shape
    dec_batch, dec_seq, _ = x_sample.shape
    kv_win = cache_k.shape[2]
    assert kv_win == HIST and seq % PROMPT_TILE == 0 and seq >= HIST
    assert dec_seq % 8 == 0 and dec_seq <= GROUP and dec_batch % SAMPLE_BATCH_PER_STEP == 0

    win = w_in[0].astype(BF16)
    wout = w_out[0].astype(BF16)
    wg = w_ple_gate[0].astype(BF16)
    wp = w_ple_proj[0].astype(BF16)
    nin = norm_in[0][None, :]
    ncv = norm_conv[0][None, :]
    nat = norm_att[0][None, :]
    npl = ple_norm[0][None, :]
    nfin = final_norm[None, :]
    cw = conv_w[0]
    bias_p, bias_s = _bias_tables(rel_bias[0], dec_seq)

    weight_specs = [
        _const_spec(win.shape), _const_spec(wout.shape), _const_spec(wg.shape), _const_spec(wp.shape),
        _const_spec(nin.shape), _const_spec(cw.shape), _const_spec(ncv.shape), _const_spec(nat.shape),
        _const_spec(npl.shape), _const_spec(nfin.shape),
    ]
    weights = (win, wout, wg, wp, nin, cw, ncv, nat, npl, nfin)

    n_t = seq // PROMPT_TILE
    state_blocks = HIST // PROMPT_TILE
    row_spec = lambda w: pl.BlockSpec((None, PROMPT_TILE, w), lambda b, t: (b, t, 0))
    state_spec = pl.BlockSpec((None, PROMPT_TILE, D_ATT),
                              lambda b, t: (b, jnp.maximum(t - (n_t - state_blocks), 0), 0))
    y_p, k_p, v_p, c_p = pl.pallas_call(
        _prompt_kernel,
        grid=(batch, n_t),
        in_specs=[row_spec(D_MODEL), row_spec(D_PLE)] + weight_specs + [_const_spec(bias_p.shape)],
        out_specs=[row_spec(D_MODEL), state_spec, state_spec,
                   pl.BlockSpec((None, 8, D_CONV), lambda b, t: (b, 0, 0))],
        out_shape=[jax.ShapeDtypeStruct((batch, seq, D_MODEL), F32),
                   jax.ShapeDtypeStruct((batch, HIST, D_ATT), F32),
                   jax.ShapeDtypeStruct((batch, HIST, D_ATT), F32),
                   jax.ShapeDtypeStruct((batch, 8, D_CONV), F32)],
        scratch_shapes=[pltpu.VMEM((HIST + PROMPT_TILE, D_ATT), BF16),
                        pltpu.VMEM((HIST + PROMPT_TILE, D_ATT), BF16),
                        pltpu.VMEM((8, D_CONV), F32)],
        compiler_params=pltpu.CompilerParams(dimension_semantics=("arbitrary", "arbitrary"),
                                             vmem_limit_bytes=VMEM_LIMIT_BYTES),
        name="prompt_layer",
    )(x_prompt, p_prompt[0], *weights, bias_p)

    nb = SAMPLE_BATCH_PER_STEP
    rows = nb * dec_seq
    flat_spec = lambda w: pl.BlockSpec((rows, w), lambda i: (i, 0))
    y_s, k_s, v_s, c_s = pl.pallas_call(
        _sample_kernel,
        grid=(dec_batch // nb,),
        in_specs=[flat_spec(D_MODEL), flat_spec(D_PLE),
                  pl.BlockSpec((nb, HIST, D_ATT), lambda i: (i, 0, 0)),
                  pl.BlockSpec((nb, HIST, D_ATT), lambda i: (i, 0, 0)),
                  pl.BlockSpec((nb, CONV_W - 1, D_CONV), lambda i: (i, 0, 0))]
                 + weight_specs + [_const_spec(bias_s.shape)],
        out_specs=[flat_spec(D_MODEL), flat_spec(D_ATT), flat_spec(D_ATT),
                   pl.BlockSpec((nb, 8, D_CONV), lambda i: (i, 0, 0))],
        out_shape=[jax.ShapeDtypeStruct((dec_batch * dec_seq, D_MODEL), F32),
                   jax.ShapeDtypeStruct((dec_batch * dec_seq, D_ATT), F32),
                   jax.ShapeDtypeStruct((dec_batch * dec_seq, D_ATT), F32),
                   jax.ShapeDtypeStruct((dec_batch, 8, D_CONV), F32)],
        scratch_shapes=[pltpu.VMEM((KWIN, D_ATT), BF16), pltpu.VMEM((KWIN, D_ATT), BF16)],
        compiler_params=pltpu.CompilerParams(dimension_semantics=("arbitrary",),
                                             vmem_limit_bytes=VMEM_LIMIT_BYTES),
        name="sample_layer",
    )(x_sample.reshape(dec_batch * dec_seq, D_MODEL), p_sample[0].reshape(dec_batch * dec_seq, D_PLE),
      cache_k[0].reshape(dec_batch, HIST, D_ATT), cache_v[0].reshape(dec_batch, HIST, D_ATT),
      state_conv[0], *weights, bias_s)

    return (y_p,
            y_s.reshape(dec_batch, dec_seq, D_MODEL),
            k_p.reshape(1, batch, HIST, N_HEADS, HEAD_DIM),
            v_p.reshape(1, batch, HIST, N_HEADS, HEAD_DIM),
            c_p[None, :, 8 - (CONV_W - 1):, :],
            k_s.reshape(1, dec_batch, dec_seq, N_HEADS, HEAD_DIM),
            v_s.reshape(1, dec_batch, dec_seq, N_HEADS, HEAD_DIM),
            c_s[None, :, 8 - (CONV_W - 1):, :])
```

```python
import functools

import jax
import jax.numpy as jnp
from jax import lax
from jax.experimental import pallas as pl
from jax.experimental.pallas import tpu as pltpu

D_MODEL = 1024
D_CONV = 512
D_ATT = 512
HEAD_DIM = 64
N_HEADS = 8
N_PAIRS = N_HEADS // 2
PAIR_W = 2 * HEAD_DIM
CHUNK = 64
N_PAST_CHUNKS = 8
HIST = N_PAST_CHUNKS * CHUNK
REL_CLIP = 128
D_PLE = 256
CONV_W = 3
EPS = 1e-6
NEG = -1e30
SCALE = HEAD_DIM ** -0.5

GROUP = 2 * CHUNK
KWIN = HIST + GROUP
PROMPT_TILE = 256
SAMPLE_BATCH_PER_STEP = 4
VMEM_LIMIT_BYTES = 56 * 1024 * 1024

F32 = jnp.float32
BF16 = jnp.bfloat16


def _rms(x, g):
    ms = jnp.mean(x * x, axis=-1, keepdims=True)
    return x * lax.rsqrt(ms + EPS) * g


def _sigmoid(x):
    return 1.0 / (1.0 + jnp.exp(-x))


def _dot(a, b):
    return jnp.dot(a, b, preferred_element_type=F32)


def _head(x, win_ref, nin_ref, cw_ref, ncv_ref, prev_rows):
    xn = _rms(x, nin_ref[...]).astype(BF16)

    def proj(i):
        return _dot(xn, win_ref[:, 512 * i:512 * (i + 1)])

    u = proj(2) * proj(0)
    u1, u2 = prev_rows(u)
    cv = cw_ref[0:1, :] * u2 + cw_ref[1:2, :] * u1 + cw_ref[2:3, :] * u
    zc = proj(3)
    yc = proj(1) * cv * (zc * _sigmoid(zc))
    yc_n = _rms(yc, ncv_ref[...])
    q = (proj(4) * SCALE).astype(BF16)
    return u, yc_n, q, proj(5), proj(6), proj(7)


def _attend(q2, kw, vw, bias, colmask):
    r = q2.shape[0]
    lane = lax.broadcasted_iota(jnp.int32, q2.shape, 1)
    zero = jnp.zeros_like(q2)
    qs = jnp.concatenate([jnp.where(lane < HEAD_DIM, q2, zero),
                          jnp.where(lane >= HEAD_DIM, q2, zero)], axis=0)
    s = lax.dot_general(qs, kw, (((1,), (1,)), ((), ())), preferred_element_type=F32)
    s = s + bias
    if colmask is not None:
        s = jnp.where(colmask, s, NEG)
    m = jnp.max(s, axis=-1, keepdims=True)
    e = jnp.exp(s - m)
    l = jnp.sum(e, axis=-1, keepdims=True)
    o = _dot(e.astype(BF16), vw) * (1.0 / l)
    lane_o = lax.broadcasted_iota(jnp.int32, (r, PAIR_W), 1)
    return jnp.where(lane_o < HEAD_DIM, o[:r], o[r:])


def _build_bias(gvec_ref, btab, n_rows, in_band):
    width = gvec_ref.shape[1]
    i = lax.broadcasted_iota(jnp.int32, (n_rows, width), 0)
    s = lax.broadcasted_iota(jnp.int32, (n_rows, KWIN), 1)
    band = in_band(lax.broadcasted_iota(jnp.int32, (n_rows, KWIN), 0), s)
    for h in range(N_HEADS):
        x = jnp.broadcast_to(gvec_ref[h:h + 1, :], (n_rows, width))
        bit = 0
        while (1 << bit) < n_rows:
            x = jnp.where(((i >> bit) & 1) == 1, pltpu.roll(x, 1 << bit, 1), x)
            bit += 1
        tb = jnp.where(band, x[:, GROUP:GROUP + KWIN], NEG)
        btab[h // 2, (h % 2) * n_rows:(h % 2 + 1) * n_rows, :] = tb


def _tail(x, yc_n, ya_n, p, wout_ref, wg_ref, wp_ref, npl_ref, nfin_ref):
    y = _dot(yc_n.astype(BF16), wout_ref[0:D_CONV, :]) + _dot(ya_n.astype(BF16), wout_ref[D_CONV:, :])
    x1 = x + y
    gate = _sigmoid(_dot(_rms(x1, npl_ref[...]).astype(BF16), wg_ref[...]))
    x2 = x1 + gate * _dot(p.astype(BF16), wp_ref[...])
    return _rms(x2, nfin_ref[...])


def _prompt_kernel(x_ref, p_ref, win_ref, wout_ref, wg_ref, wp_ref, nin_ref, cw_ref, ncv_ref,
                   nat_ref, npl_ref, nfin_ref, gvec_ref,
                   y_ref, ks_ref, vs_ref, cs_ref, kbuf, vbuf, ucar, bias_ref):
    t = pl.program_id(1)
    n_t = pl.num_programs(1)
    tile = x_ref.shape[0]

    @pl.when((pl.program_id(0) == 0) & (t == 0))
    def _():
        _build_bias(gvec_ref, bias_ref, GROUP,
                    lambda i, s: ((i < CHUNK) & (s < HIST + CHUNK)) | ((i >= CHUNK) & (s >= CHUNK)))

    @pl.when(t == 0)
    def _():
        kbuf[0:HIST, :] = jnp.zeros((HIST, D_ATT), BF16)
        vbuf[0:HIST, :] = jnp.zeros((HIST, D_ATT), BF16)
        ucar[...] = jnp.zeros(ucar.shape, F32)

    x = x_ref[...]
    row = lax.broadcasted_iota(jnp.int32, (tile, D_CONV), 0)

    def prev_rows(u):
        c1 = ucar[7:8, :]
        c2 = ucar[6:7, :]
        u1 = jnp.where(row == 0, c1, pltpu.roll(u, 1, 0))
        u2 = jnp.where(row == 0, c2, jnp.where(row == 1, c1, pltpu.roll(u, 2, 0)))
        return u1, u2

    u, yc_n, q, k, v, za = _head(x, win_ref, nin_ref, cw_ref, ncv_ref, prev_rows)
    ucar[...] = u[tile - 8:, :]
    kbuf[HIST:, :] = k.astype(BF16)
    vbuf[HIST:, :] = v.astype(BF16)

    @pl.when(t >= n_t - HIST // tile)
    def _():
        ks_ref[...] = k
        vs_ref[...] = v

    @pl.when(t == n_t - 1)
    def _():
        cs_ref[...] = u[tile - 8:, :]

    col = lax.broadcasted_iota(jnp.int32, (2 * GROUP, KWIN), 1)
    groups = []
    for g in range(tile // GROUP):
        colmask = col >= HIST - t * tile - g * GROUP
        pairs = []
        for p in range(N_PAIRS):
            lanes = slice(p * PAIR_W, (p + 1) * PAIR_W)
            pairs.append(_attend(q[g * GROUP:(g + 1) * GROUP, lanes],
                                 kbuf[g * GROUP:g * GROUP + KWIN, lanes],
                                 vbuf[g * GROUP:g * GROUP + KWIN, lanes],
                                 bias_ref[p], colmask))
        groups.append(jnp.concatenate(pairs, axis=1))
    ya = jnp.concatenate(groups, axis=0) * (za * _sigmoid(za))
    ya_n = _rms(ya, nat_ref[...])

    y_ref[...] = _tail(x, yc_n, ya_n, p_ref[...], wout_ref, wg_ref, wp_ref, npl_ref, nfin_ref)

    k_keep = kbuf[tile:, :]
    v_keep = vbuf[tile:, :]
    kbuf[0:HIST, :] = k_keep
    vbuf[0:HIST, :] = v_keep


def _sample_kernel(x_ref, p_ref, ck_ref, cv_ref, sc_ref, win_ref, wout_ref, wg_ref, wp_ref, nin_ref,
                   cw_ref, ncv_ref, nat_ref, npl_ref, nfin_ref, gvec_ref,
                   y_ref, ks_ref, vs_ref, cs_ref, kbuf, vbuf, bias_ref):
    nb = ck_ref.shape[0]
    rows = x_ref.shape[0]
    seq = rows // nb

    @pl.when(pl.program_id(0) == 0)
    def _():
        _build_bias(gvec_ref, bias_ref, seq, lambda i, s: s < HIST + seq)
        kbuf[HIST + seq:, :] = jnp.zeros((KWIN - HIST - seq, D_ATT), BF16)
        vbuf[HIST + seq:, :] = jnp.zeros((KWIN - HIST - seq, D_ATT), BF16)

    x = x_ref[...]
    rmod = lax.broadcasted_iota(jnp.int32, (rows, D_CONV), 0) % seq

    def prev_rows(u):
        e1 = jnp.concatenate([jnp.broadcast_to(sc_ref[j, 1:2, :], (seq, D_CONV)) for j in range(nb)], axis=0)
        e2 = jnp.concatenate([jnp.broadcast_to(sc_ref[j, 0:1, :], (seq, D_CONV)) for j in range(nb)], axis=0)
        u1 = jnp.where(rmod == 0, e1, pltpu.roll(u, 1, 0))
        u2 = jnp.where(rmod == 0, e2, jnp.where(rmod == 1, e1, pltpu.roll(u, 2, 0)))
        return u1, u2

    u, yc_n, q, k, v, za = _head(x, win_ref, nin_ref, cw_ref, ncv_ref, prev_rows)
    ks_ref[...] = k
    vs_ref[...] = v
    kb = k.astype(BF16)
    vb = v.astype(BF16)

    batches = []
    for j in range(nb):
        rs = slice(j * seq, (j + 1) * seq)
        cs_ref[j] = u[(j + 1) * seq - 8:(j + 1) * seq, :]
        kbuf[0:HIST, :] = ck_ref[j].astype(BF16)
        vbuf[0:HIST, :] = cv_ref[j].astype(BF16)
        kbuf[HIST:HIST + seq, :] = kb[rs, :]
        vbuf[HIST:HIST + seq, :] = vb[rs, :]
        pairs = []
        for p in range(N_PAIRS):
            lanes = slice(p * PAIR_W, (p + 1) * PAIR_W)
            pairs.append(_attend(q[rs, lanes], kbuf[:, lanes], vbuf[:, lanes], bias_ref[p], None))
        batches.append(jnp.concatenate(pairs, axis=1))
    ya = jnp.concatenate(batches, axis=0) * (za * _sigmoid(za))
    ya_n = _rms(ya, nat_ref[...])

    y_ref[...] = _tail(x, yc_n, ya_n, p_ref[...], wout_ref, wg_ref, wp_ref, npl_ref, nfin_ref)


def _bias_vector(rel_bias):
    width = KWIN + GROUP
    flat = KWIN - REL_CLIP
    assert width - flat == 2 * REL_CLIP
    return jnp.concatenate([jnp.broadcast_to(rel_bias[:, 2 * REL_CLIP:], (N_HEADS, flat)),
                            rel_bias[:, ::-1][:, :2 * REL_CLIP]], axis=1)


def _const_spec(shape):
    nd = len(shape)
    return pl.BlockSpec(shape, lambda *_: (0,) * nd, pipeline_mode=pl.Buffered(1))


def kernel(x_prompt, x_sample, cache_k, cache_v, state_conv, p_prompt, p_sample, norm_in, w_in, conv_w,
           rel_bias, norm_conv, norm_att, w_out, ple_norm, w_ple_gate, w_ple_proj, final_norm):
    depth = w_in.shape[0]
    assert depth == 1, "single-layer kernel"
    batch, seq, _ = x_prompt.shape
    dec_batch, dec_seq, _ = x_sample.shape
    kv_win = cache_k.shape[2]
    assert kv_win == HIST and seq % PROMPT_TILE == 0 and seq >= HIST
    assert dec_seq % 8 == 0 and dec_seq <= GROUP and dec_batch % SAMPLE_BATCH_PER_STEP == 0

    win = w_in[0].astype(BF16)
    wout = w_out[0].astype(BF16)
    wg = w_ple_gate[0].astype(BF16)
    wp = w_ple_proj[0].astype(BF16)
    nin = norm_in[0][None, :]
    ncv = norm_conv[0][None, :]
    nat = norm_att[0][None, :]
    npl = ple_norm[0][None, :]
    nfin = final_norm[None, :]
    cw = conv_w[0]
    gvec = _bias_vector(rel_bias[0])

    weight_specs = [
        _const_spec(win.shape), _const_spec(wout.shape), _const_spec(wg.shape), _const_spec(wp.shape),
        _const_spec(nin.shape), _const_spec(cw.shape), _const_spec(ncv.shape), _const_spec(nat.shape),
        _const_spec(npl.shape), _const_spec(nfin.shape),
    ]
    weights = (win, wout, wg, wp, nin, cw, ncv, nat, npl, nfin)

    n_t = seq // PROMPT_TILE
    state_blocks = HIST // PROMPT_TILE
    row_spec = lambda w: pl.BlockSpec((None, PROMPT_TILE, w), lambda b, t: (b, t, 0))
    state_spec = pl.BlockSpec((None, PROMPT_TILE, D_ATT),
                              lambda b, t: (b, jnp.maximum(t - (n_t - state_blocks), 0), 0))
    y_p, k_p, v_p, c_p = pl.pallas_call(
        _prompt_kernel,
        grid=(batch, n_t),
        in_specs=[row_spec(D_MODEL), row_spec(D_PLE)] + weight_specs + [_const_spec(gvec.shape)],
        out_specs=[row_spec(D_MODEL), state_spec, state_spec,
                   pl.BlockSpec((None, 8, D_CONV), lambda b, t: (b, 0, 0))],
        out_shape=[jax.ShapeDtypeStruct((batch, seq, D_MODEL), F32),
                   jax.ShapeDtypeStruct((batch, HIST, D_ATT), F32),
                   jax.ShapeDtypeStruct((batch, HIST, D_ATT), F32),
                   jax.ShapeDtypeStruct((batch, 8, D_CONV), F32)],
        scratch_shapes=[pltpu.VMEM((HIST + PROMPT_TILE, D_ATT), BF16),
                        pltpu.VMEM((HIST + PROMPT_TILE, D_ATT), BF16),
                        pltpu.VMEM((8, D_CONV), F32),
                        pltpu.VMEM((N_PAIRS, 2 * GROUP, KWIN), F32)],
        compiler_params=pltpu.CompilerParams(dimension_semantics=("arbitrary", "arbitrary"),
                                             vmem_limit_bytes=VMEM_LIMIT_BYTES),
        name="prompt_layer",
    )(x_prompt, p_prompt[0], *weights, gvec)

    nb = SAMPLE_BATCH_PER_STEP
    rows = nb * dec_seq
    flat_spec = lambda w: pl.BlockSpec((rows, w), lambda i: (i, 0))
    y_s, k_s, v_s, c_s = pl.pallas_call(
        _sample_kernel,
        grid=(dec_batch // nb,),
        in_specs=[flat_spec(D_MODEL), flat_spec(D_PLE),
                  pl.BlockSpec((nb, HIST, D_ATT), lambda i: (i, 0, 0)),
                  pl.BlockSpec((nb, HIST, D_ATT), lambda i: (i, 0, 0)),
                  pl.BlockSpec((nb, CONV_W - 1, D_CONV), lambda i: (i, 0, 0))]
                 + weight_specs + [_const_spec(gvec.shape)],
        out_specs=[flat_spec(D_MODEL), flat_spec(D_ATT), flat_spec(D_ATT),
                   pl.BlockSpec((nb, 8, D_CONV), lambda i: (i, 0, 0))],
        out_shape=[jax.ShapeDtypeStruct((dec_batch * dec_seq, D_MODEL), F32),
                   jax.ShapeDtypeStruct((dec_batch * dec_seq, D_ATT), F32),
                   jax.ShapeDtypeStruct((dec_batch * dec_seq, D_ATT), F32),
                   jax.ShapeDtypeStruct((dec_batch, 8, D_CONV), F32)],
        scratch_shapes=[pltpu.VMEM((KWIN, D_ATT), BF16), pltpu.VMEM((KWIN, D_ATT), BF16),
                        pltpu.VMEM((N_PAIRS, 2 * dec_seq, KWIN), F32)],
        compiler_params=pltpu.CompilerParams(dimension_semantics=("arbitrary",),
                                             vmem_limit_bytes=VMEM_LIMIT_BYTES),
        name="sample_layer",
    )(x_sample.reshape(dec_batch * dec_seq, D_MODEL), p_sample[0].reshape(dec_batch * dec_seq, D_PLE),
      cache_k[0].reshape(dec_batch, HIST, D_ATT), cache_v[0].reshape(dec_batch, HIST, D_ATT),
      state_conv[0], *weights, gvec)

    return (y_p,
            y_s.reshape(dec_batch, dec_seq, D_MODEL),
            k_p.reshape(1, batch, HIST, N_HEADS, HEAD_DIM),
            v_p.reshape(1, batch, HIST, N_HEADS, HEAD_DIM),
            c_p[None, :, 8 - (CONV_W - 1):, :],
            k_s.reshape(1, dec_batch, dec_seq, N_HEADS, HEAD_DIM),
            v_s.reshape(1, dec_batch, dec_seq, N_HEADS, HEAD_DIM),
            c_s[None, :, 8 - (CONV_W - 1):, :])
```

```python
import jax
import jax.numpy as jnp
from jax import lax
from jax.experimental import pallas as pl
from jax.experimental.pallas import tpu as pltpu

D_MODEL = 1024
D_CONV = 512
D_ATT = 512
HEAD_DIM = 64
N_HEADS = 8
N_PAIRS = N_HEADS // 2
LANES = 128
PAIR_W = 2 * HEAD_DIM
CHUNK = 64
N_PAST_CHUNKS = 8
HIST = N_PAST_CHUNKS * CHUNK
REL_CLIP = 128
D_PLE = 256
CONV_W = 3
EPS = 1e-6
NEG = -1e30
SCALE = HEAD_DIM ** -0.5

GROUP = 2 * CHUNK
KWIN = HIST + GROUP
KBLOCKS = KWIN // LANES
PROMPT_TILE = 512
HALF = 256
SAMPLE_BATCH_PER_STEP = 4
VMEM_LIMIT_BYTES = 56 * 1024 * 1024

F32 = jnp.float32
BF16 = jnp.bfloat16


def _rms(x, g):
    ms = jnp.mean(x * x, axis=-1, keepdims=True)
    return x * lax.rsqrt(ms + EPS) * g


def _sigmoid(x):
    return 1.0 / (1.0 + jnp.exp(-x))


def _dot(a, b):
    return jnp.dot(a, b, preferred_element_type=F32)


def _run(steps):
    try:
        while True:
            next(steps)
    except StopIteration as done:
        return done.value


def _zip(main, side):
    side_val, side_done = None, False
    while True:
        try:
            next(main)
        except StopIteration as done:
            main_val = done.value
            break
        if not side_done:
            try:
                next(side)
            except StopIteration as done:
                side_val, side_done = done.value, True
    if not side_done:
        side_val = _run(side)
    return main_val, side_val


def _head(x, win_ref, nin_ref, cw_ref, ncv_ref, prev_rows):
    xn = _rms(x, nin_ref[...]).astype(BF16)

    def proj(i):
        return _dot(xn, win_ref[:, 512 * i:512 * (i + 1)])

    h = proj(0)
    yield
    u = proj(2) * h
    u1, u2 = prev_rows(u)
    cv = cw_ref[0:1, :] * u2 + cw_ref[1:2, :] * u1 + cw_ref[2:3, :] * u
    yield
    zc = proj(3)
    gz = zc * _sigmoid(zc)
    yield
    yc = proj(1) * cv * gz
    yc_n = _rms(yc, ncv_ref[...])
    yield
    q = (proj(4) * SCALE).astype(BF16)
    yield
    k = proj(5)
    yield
    v = proj(6)
    yield
    za = proj(7)
    return u, yc_n, q, k, v, za


def _attend(q2, kw, vw, bias):
    r = q2.shape[0]
    lane = lax.broadcasted_iota(jnp.int32, q2.shape, 1)
    zero = jnp.zeros_like(q2)
    qs = jnp.concatenate([jnp.where(lane < HEAD_DIM, q2, zero),
                          jnp.where(lane >= HEAD_DIM, q2, zero)], axis=0)
    s = lax.dot_general(qs, kw, (((1,), (1,)), ((), ())), preferred_element_type=F32)
    s = s + bias
    m = jnp.max(s, axis=-1, keepdims=True)
    e = jnp.exp(s - m)
    l = jnp.sum(e, axis=-1, keepdims=True)
    o = _dot(e.astype(BF16), vw) * (1.0 / l)
    lane_o = lax.broadcasted_iota(jnp.int32, (r, PAIR_W), 1)
    return jnp.where(lane_o < HEAD_DIM, o[:r], o[r:])


def _build_bias(gvec_ref, n_rows, in_band, store):
    width = gvec_ref.shape[1]
    i = lax.broadcasted_iota(jnp.int32, (n_rows, width), 0)
    band = in_band(lax.broadcasted_iota(jnp.int32, (n_rows, KWIN), 0),
                   lax.broadcasted_iota(jnp.int32, (n_rows, KWIN), 1))
    for h in range(N_HEADS):
        x = jnp.broadcast_to(gvec_ref[h:h + 1, :], (n_rows, width))
        bit = 0
        while (1 << bit) < n_rows:
            x = jnp.where(((i >> bit) & 1) == 1, pltpu.roll(x, 1 << bit, 1), x)
            bit += 1
        store(h, jnp.where(band, x[:, GROUP:GROUP + KWIN], NEG))


def _tail(x, yc_n, ya_n, p, wout_ref, wg_ref, wp_ref, npl_ref, nfin_ref):
    pe = _dot(p.astype(BF16), wp_ref[...])
    yield
    y = _dot(yc_n.astype(BF16), wout_ref[0:D_CONV, :]) + _dot(ya_n.astype(BF16), wout_ref[D_CONV:, :])
    x1 = x + y
    yield
    gate = _sigmoid(_dot(_rms(x1, npl_ref[...]).astype(BF16), wg_ref[...]))
    x2 = x1 + gate * pe
    return _rms(x2, nfin_ref[...])


def _prompt_kernel(x_ref, p_ref, win_ref, wout_ref, wg_ref, wp_ref, nin_ref, cw_ref, ncv_ref,
                   nat_ref, npl_ref, nfin_ref, gvec_ref,
                   y_ref, ks_ref, vs_ref, cs_ref, kbuf, vbuf, ucar, bias_ref):
    t = pl.program_id(1)
    tile = x_ref.shape[0]

    @pl.when((pl.program_id(0) == 0) & (t == 0))
    def _():
        def store(h, tb):
            for cb in range(KBLOCKS):
                bias_ref[h // 2, cb, (h % 2) * GROUP:(h % 2 + 1) * GROUP, :] = tb[:, cb * LANES:(cb + 1) * LANES]

        _build_bias(gvec_ref, GROUP,
                    lambda i, s: ((i < CHUNK) & (s < HIST + CHUNK)) | ((i >= CHUNK) & (s >= CHUNK)), store)
        bias_ref[N_PAIRS] = jnp.full((KBLOCKS, 2 * GROUP, LANES), NEG, F32)

    @pl.when(t == 0)
    def _():
        kbuf[0:HIST, :] = jnp.zeros((HIST, D_ATT), BF16)
        vbuf[0:HIST, :] = jnp.zeros((HIST, D_ATT), BF16)
        ucar[...] = jnp.zeros(ucar.shape, F32)

    n_half = tile // HALF
    row = lax.broadcasted_iota(jnp.int32, (HALF, D_CONV), 0)

    def front(h, c1, c2):
        rows = slice(h * HALF, (h + 1) * HALF)

        def prev_rows(u):
            u1 = jnp.where(row == 0, c1, pltpu.roll(u, 1, 0))
            u2 = jnp.where(row == 0, c2, jnp.where(row == 1, c1, pltpu.roll(u, 2, 0)))
            return u1, u2

        u, yc_n, q, k, v, za = yield from _head(x_ref[rows, :], win_ref, nin_ref, cw_ref, ncv_ref, prev_rows)
        kbuf[HIST + h * HALF:HIST + (h + 1) * HALF, :] = k.astype(BF16)
        vbuf[HIST + h * HALF:HIST + (h + 1) * HALF, :] = v.astype(BF16)
        ks_ref[rows, :] = k
        vs_ref[rows, :] = v
        return u, yc_n, q, za

    def attention(h, q, za):
        groups = []
        for gl in range(HALF // GROUP):
            g = h * (HALF // GROUP) + gl
            first_key = t * tile + g * GROUP - HIST
            pairs = []
            for p in range(N_PAIRS):
                lanes = slice(p * PAIR_W, (p + 1) * PAIR_W)
                bias = jnp.concatenate(
                    [bias_ref[jnp.where(first_key + cb * LANES < 0, N_PAIRS, p), cb] for cb in range(KBLOCKS)],
                    axis=1)
                pairs.append(_attend(q[gl * GROUP:(gl + 1) * GROUP, lanes],
                                     kbuf[g * GROUP:g * GROUP + KWIN, lanes],
                                     vbuf[g * GROUP:g * GROUP + KWIN, lanes], bias))
                yield
            groups.append(jnp.concatenate(pairs, axis=1))
        ya = jnp.concatenate(groups, axis=0) * (za * _sigmoid(za))
        return _rms(ya, nat_ref[...])

    def tail(h, yc_n, ya_n):
        rows = slice(h * HALF, (h + 1) * HALF)
        y_ref[rows, :] = yield from _tail(x_ref[rows, :], yc_n, ya_n, p_ref[rows, :],
                                          wout_ref, wg_ref, wp_ref, npl_ref, nfin_ref)

    u, yc_n, q, za = _run(front(0, ucar[7:8, :], ucar[6:7, :]))
    pending_tail = None
    for h in range(n_half):
        if h + 1 < n_half:
            side = front(h + 1, u[HALF - 1:HALF, :], u[HALF - 2:HALF - 1, :])
        else:
            side = pending_tail
            pending_tail = None
        if pending_tail is not None:
            _run(pending_tail)
        ya_n, nxt = _zip(attention(h, q, za), side) if side is not None else (_run(attention(h, q, za)), None)
        pending_tail = tail(h, yc_n, ya_n)
        if h + 1 < n_half:
            u, yc_n, q, za = nxt
    _run(pending_tail)
    ucar[...] = u[HALF - 8:, :]
    cs_ref[...] = u[HALF - 8:, :]

    k_keep = kbuf[tile:, :]
    v_keep = vbuf[tile:, :]
    kbuf[0:HIST, :] = k_keep
    vbuf[0:HIST, :] = v_keep


def _sample_kernel(x_ref, p_ref, ck_ref, cv_ref, sc_ref, win_ref, wout_ref, wg_ref, wp_ref, nin_ref,
                   cw_ref, ncv_ref, nat_ref, npl_ref, nfin_ref, gvec_ref,
                   y_ref, ks_ref, vs_ref, cs_ref, kbuf, vbuf, bias_ref):
    nb = ck_ref.shape[0]
    rows = x_ref.shape[0]
    seq = rows // nb

    @pl.when(pl.program_id(0) == 0)
    def _():
        def store(h, tb):
            bias_ref[h // 2, (h % 2) * seq:(h % 2 + 1) * seq, :] = tb

        _build_bias(gvec_ref, seq, lambda i, s: s < HIST + seq, store)
        kbuf[HIST + seq:, :] = jnp.zeros((KWIN - HIST - seq, D_ATT), BF16)
        vbuf[HIST + seq:, :] = jnp.zeros((KWIN - HIST - seq, D_ATT), BF16)

    x = x_ref[...]
    rmod = lax.broadcasted_iota(jnp.int32, (rows, D_CONV), 0) % seq

    def prev_rows(u):
        e1 = jnp.concatenate([jnp.broadcast_to(sc_ref[j, 1:2, :], (seq, D_CONV)) for j in range(nb)], axis=0)
        e2 = jnp.concatenate([jnp.broadcast_to(sc_ref[j, 0:1, :], (seq, D_CONV)) for j in range(nb)], axis=0)
        u1 = jnp.where(rmod == 0, e1, pltpu.roll(u, 1, 0))
        u2 = jnp.where(rmod == 0, e2, jnp.where(rmod == 1, e1, pltpu.roll(u, 2, 0)))
        return u1, u2

    u, yc_n, q, k, v, za = _run(_head(x, win_ref, nin_ref, cw_ref, ncv_ref, prev_rows))
    ks_ref[...] = k
    vs_ref[...] = v
    kb = k.astype(BF16)
    vb = v.astype(BF16)

    batches = []
    for j in range(nb):
        rs = slice(j * seq, (j + 1) * seq)
        cs_ref[j] = u[(j + 1) * seq - 8:(j + 1) * seq, :]
        kbuf[0:HIST, :] = ck_ref[j].astype(BF16)
        vbuf[0:HIST, :] = cv_ref[j].astype(BF16)
        kbuf[HIST:HIST + seq, :] = kb[rs, :]
        vbuf[HIST:HIST + seq, :] = vb[rs, :]
        pairs = []
        for p in range(N_PAIRS):
            lanes = slice(p * PAIR_W, (p + 1) * PAIR_W)
            pairs.append(_attend(q[rs, lanes], kbuf[:, lanes], vbuf[:, lanes], bias_ref[p]))
        batches.append(jnp.concatenate(pairs, axis=1))
    ya = jnp.concatenate(batches, axis=0) * (za * _sigmoid(za))
    ya_n = _rms(ya, nat_ref[...])

    y_ref[...] = _run(_tail(x, yc_n, ya_n, p_ref[...], wout_ref, wg_ref, wp_ref, npl_ref, nfin_ref))


def _bias_vector(rel_bias):
    width = KWIN + GROUP
    flat = KWIN - REL_CLIP
    assert width - flat == 2 * REL_CLIP
    return jnp.concatenate([jnp.broadcast_to(rel_bias[:, 2 * REL_CLIP:], (N_HEADS, flat)),
                            rel_bias[:, ::-1][:, :2 * REL_CLIP]], axis=1)


def _const_spec(shape):
    nd = len(shape)
    return pl.BlockSpec(shape, lambda *_: (0,) * nd, pipeline_mode=pl.Buffered(1))


def kernel(x_prompt, x_sample, cache_k, cache_v, state_conv, p_prompt, p_sample, norm_in, w_in, conv_w,
           rel_bias, norm_conv, norm_att, w_out, ple_norm, w_ple_gate, w_ple_proj, final_norm):
    depth = w_in.shape[0]
    assert depth == 1, "single-layer kernel"
    batch, seq, _ = x_prompt.shape
    dec_batch, dec_seq, _ = x_sample.shape
    kv_win = cache_k.shape[2]
    assert kv_win == HIST and PROMPT_TILE == HIST and seq % PROMPT_TILE == 0
    assert PROMPT_TILE % HALF == 0 and HALF % GROUP == 0
    assert dec_seq % 8 == 0 and dec_seq <= GROUP and dec_batch % SAMPLE_BATCH_PER_STEP == 0

    win = w_in[0].astype(BF16)
    wout = w_out[0].astype(BF16)
    wg = w_ple_gate[0].astype(BF16)
    wp = w_ple_proj[0].astype(BF16)
    nin = norm_in[0][None, :]
    ncv = norm_conv[0][None, :]
    nat = norm_att[0][None, :]
    npl = ple_norm[0][None, :]
    nfin = final_norm[None, :]
    cw = conv_w[0]
    gvec = _bias_vector(rel_bias[0])

    weight_specs = [
        _const_spec(win.shape), _const_spec(wout.shape), _const_spec(wg.shape), _const_spec(wp.shape),
        _const_spec(nin.shape), _const_spec(cw.shape), _const_spec(ncv.shape), _const_spec(nat.shape),
        _const_spec(npl.shape), _const_spec(nfin.shape),
    ]
    weights = (win, wout, wg, wp, nin, cw, ncv, nat, npl, nfin)

    n_t = seq // PROMPT_TILE
    row_spec = lambda w: pl.BlockSpec((None, PROMPT_TILE, w), lambda b, t: (b, t, 0))
    state_spec = pl.BlockSpec((None, HIST, D_ATT), lambda b, t: (b, 0, 0))
    y_p, k_p, v_p, c_p = pl.pallas_call(
        _prompt_kernel,
        grid=(batch, n_t),
        in_specs=[row_spec(D_MODEL), row_spec(D_PLE)] + weight_specs + [_const_spec(gvec.shape)],
        out_specs=[row_spec(D_MODEL), state_spec, state_spec,
                   pl.BlockSpec((None, 8, D_CONV), lambda b, t: (b, 0, 0))],
        out_shape=[jax.ShapeDtypeStruct((batch, seq, D_MODEL), F32),
                   jax.ShapeDtypeStruct((batch, HIST, D_ATT), F32),
                   jax.ShapeDtypeStruct((batch, HIST, D_ATT), F32),
                   jax.ShapeDtypeStruct((batch, 8, D_CONV), F32)],
        scratch_shapes=[pltpu.VMEM((HIST + PROMPT_TILE, D_ATT), BF16),
                        pltpu.VMEM((HIST + PROMPT_TILE, D_ATT), BF16),
                        pltpu.VMEM((8, D_CONV), F32),
                        pltpu.VMEM((N_PAIRS + 1, KBLOCKS, 2 * GROUP, LANES), F32)],
        compiler_params=pltpu.CompilerParams(dimension_semantics=("arbitrary", "arbitrary"),
                                             vmem_limit_bytes=VMEM_LIMIT_BYTES),
        name="prompt_layer",
    )(x_prompt, p_prompt[0], *weights, gvec)

    nb = SAMPLE_BATCH_PER_STEP
    rows = nb * dec_seq
    flat_spec = lambda w: pl.BlockSpec((rows, w), lambda i: (i, 0))
    y_s, k_s, v_s, c_s = pl.pallas_call(
        _sample_kernel,
        grid=(dec_batch // nb,),
        in_specs=[flat_spec(D_MODEL), flat_spec(D_PLE),
                  pl.BlockSpec((nb, HIST, D_ATT), lambda i: (i, 0, 0)),
                  pl.BlockSpec((nb, HIST, D_ATT), lambda i: (i, 0, 0)),
                  pl.BlockSpec((nb, CONV_W - 1, D_CONV), lambda i: (i, 0, 0))]
                 + weight_specs + [_const_spec(gvec.shape)],
        out_specs=[flat_spec(D_MODEL), flat_spec(D_ATT), flat_spec(D_ATT),
                   pl.BlockSpec((nb, 8, D_CONV), lambda i: (i, 0, 0))],
        out_shape=[jax.ShapeDtypeStruct((dec_batch * dec_seq, D_MODEL), F32),
                   jax.ShapeDtypeStruct((dec_batch * dec_seq, D_ATT), F32),
                   jax.ShapeDtypeStruct((dec_batch * dec_seq, D_ATT), F32),
                   jax.ShapeDtypeStruct((dec_batch, 8, D_CONV), F32)],
        scratch_shapes=[pltpu.VMEM((KWIN, D_ATT), BF16), pltpu.VMEM((KWIN, D_ATT), BF16),
                        pltpu.VMEM((N_PAIRS, 2 * dec_seq, KWIN), F32)],
        compiler_params=pltpu.CompilerParams(dimension_semantics=("arbitrary",),
                                             vmem_limit_bytes=VMEM_LIMIT_BYTES),
        name="sample_layer",
    )(x_sample.reshape(dec_batch * dec_seq, D_MODEL), p_sample[0].reshape(dec_batch * dec_seq, D_PLE),
      cache_k[0].reshape(dec_batch, HIST, D_ATT), cache_v[0].reshape(dec_batch, HIST, D_ATT),
      state_conv[0], *weights, gvec)

    return (y_p,
            y_s.reshape(dec_batch, dec_seq, D_MODEL),
            k_p.reshape(1, batch, HIST, N_HEADS, HEAD_DIM),
            v_p.reshape(1, batch, HIST, N_HEADS, HEAD_DIM),
            c_p[None, :, 8 - (CONV_W - 1):, :],
            k_s.reshape(1, dec_batch, dec_seq, N_HEADS, HEAD_DIM),
            v_s.reshape(1, dec_batch, dec_seq, N_HEADS, HEAD_DIM),
            c_s[None, :, 8 - (CONV_W - 1):, :])
```

```python
import jax
import jax.numpy as jnp
from jax import lax
from jax.experimental import pallas as pl
from jax.experimental.pallas import tpu as pltpu

D_MODEL = 1024
D_CONV = 512
D_ATT = 512
HEAD_DIM = 64
HEAD_SHIFT = 6
N_HEADS = 8
SLAB_HEADS = 4
N_SLABS = N_HEADS // SLAB_HEADS
LANES = 128
SLAB_W = SLAB_HEADS * HEAD_DIM
CHUNK = 64
N_PAST_CHUNKS = 8
HIST = N_PAST_CHUNKS * CHUNK
REL_CLIP = 128
D_PLE = 256
CONV_W = 3
EPS = 1e-6
NEG = -1e30
SCALE = HEAD_DIM ** -0.5

GROUP = 2 * CHUNK
KWIN = HIST + GROUP
KBLOCKS = KWIN // LANES
PROMPT_TILE = 512
HALF = 256
SAMPLE_BATCH_PER_STEP = 4
VMEM_LIMIT_BYTES = 56 * 1024 * 1024

F32 = jnp.float32
BF16 = jnp.bfloat16


def _rms(x, g):
    ms = jnp.mean(x * x, axis=-1, keepdims=True)
    return x * lax.rsqrt(ms + EPS) * g


def _sigmoid(x):
    return 1.0 / (1.0 + jnp.exp(-x))


def _dot(a, b):
    return jnp.dot(a, b, preferred_element_type=F32)


def _run(steps):
    try:
        while True:
            next(steps)
    except StopIteration as done:
        return done.value


def _zip(main, side):
    side_val, side_done = None, False
    while True:
        try:
            next(main)
        except StopIteration as done:
            main_val = done.value
            break
        if not side_done:
            try:
                next(side)
            except StopIteration as done:
                side_val, side_done = done.value, True
    if not side_done:
        side_val = _run(side)
    return main_val, side_val


def _head(x, win_ref, nin_ref, cw_ref, ncv_ref, prev_rows):
    xn = _rms(x, nin_ref[...]).astype(BF16)

    def proj(i):
        return _dot(xn, win_ref[:, 512 * i:512 * (i + 1)])

    h = proj(0)
    yield
    u = proj(2) * h
    u1, u2 = prev_rows(u)
    cv = cw_ref[0:1, :] * u2 + cw_ref[1:2, :] * u1 + cw_ref[2:3, :] * u
    yield
    zc = proj(3)
    gz = zc * _sigmoid(zc)
    yield
    yc = proj(1) * cv * gz
    yc_n = _rms(yc, ncv_ref[...])
    yield
    q = (proj(4) * SCALE).astype(BF16)
    yield
    k = proj(5)
    yield
    v = proj(6)
    yield
    za = proj(7)
    return u, yc_n, q, k, v, za


def _attend(qslab, kw, vw, bias):
    r = qslab.shape[0]
    lane = lax.broadcasted_iota(jnp.int32, qslab.shape, 1) >> HEAD_SHIFT
    zero = jnp.zeros_like(qslab)
    qs = jnp.concatenate([jnp.where(lane == j, qslab, zero) for j in range(SLAB_HEADS)], axis=0)
    s = lax.dot_general(qs, kw, (((1,), (1,)), ((), ())), preferred_element_type=F32)
    s = s + bias
    m = jnp.max(s, axis=-1, keepdims=True)
    e = jnp.exp(s - m)
    l = jnp.sum(e, axis=-1, keepdims=True)
    o = _dot(e.astype(BF16), vw) * (1.0 / l)
    out = o[(SLAB_HEADS - 1) * r:]
    for j in reversed(range(SLAB_HEADS - 1)):
        out = jnp.where(lane <= j, o[j * r:(j + 1) * r], out)
    return out


def _build_bias(gvec_ref, n_rows, in_band, store):
    width = gvec_ref.shape[1]
    i = lax.broadcasted_iota(jnp.int32, (n_rows, width), 0)
    band = in_band(lax.broadcasted_iota(jnp.int32, (n_rows, KWIN), 0),
                   lax.broadcasted_iota(jnp.int32, (n_rows, KWIN), 1))
    for h in range(N_HEADS):
        x = jnp.broadcast_to(gvec_ref[h:h + 1, :], (n_rows, width))
        bit = 0
        while (1 << bit) < n_rows:
            x = jnp.where(((i >> bit) & 1) == 1, pltpu.roll(x, 1 << bit, 1), x)
            bit += 1
        store(h, jnp.where(band, x[:, GROUP:GROUP + KWIN], NEG))


def _tail(x, yc_n, ya_n, p, wout_ref, wg_ref, wp_ref, npl_ref, nfin_ref):
    pe = _dot(p.astype(BF16), wp_ref[...])
    yield
    y = _dot(yc_n.astype(BF16), wout_ref[0:D_CONV, :]) + _dot(ya_n.astype(BF16), wout_ref[D_CONV:, :])
    x1 = x + y
    yield
    gate = _sigmoid(_dot(_rms(x1, npl_ref[...]).astype(BF16), wg_ref[...]))
    x2 = x1 + gate * pe
    return _rms(x2, nfin_ref[...])


def _prompt_kernel(x_ref, p_ref, win_ref, wout_ref, wg_ref, wp_ref, nin_ref, cw_ref, ncv_ref,
                   nat_ref, npl_ref, nfin_ref, gvec_ref,
                   y_ref, ks_ref, vs_ref, cs_ref, kbuf, vbuf, ucar, unext, bias_ref):
    t = pl.program_id(1)
    tile = x_ref.shape[0]

    @pl.when((pl.program_id(0) == 0) & (t == 0))
    def _():
        def store(h, tb):
            j = h % SLAB_HEADS
            for cb in range(KBLOCKS):
                bias_ref[h // SLAB_HEADS, cb, j * GROUP:(j + 1) * GROUP, :] = tb[:, cb * LANES:(cb + 1) * LANES]

        _build_bias(gvec_ref, GROUP,
                    lambda i, s: ((i < CHUNK) & (s < HIST + CHUNK)) | ((i >= CHUNK) & (s >= CHUNK)), store)
        bias_ref[N_SLABS] = jnp.full((KBLOCKS, SLAB_HEADS * GROUP, LANES), NEG, F32)

    @pl.when(t == 0)
    def _():
        kbuf[0:HIST, :] = jnp.zeros((HIST, D_ATT), BF16)
        vbuf[0:HIST, :] = jnp.zeros((HIST, D_ATT), BF16)
        ucar[...] = jnp.zeros(ucar.shape, F32)

    @pl.when(t > 0)
    def _():
        kbuf[0:HIST, :] = kbuf[tile:, :]
        vbuf[0:HIST, :] = vbuf[tile:, :]
        ucar[...] = unext[...]

    n_half = tile // HALF
    row = lax.broadcasted_iota(jnp.int32, (HALF, D_CONV), 0)

    def front(h, c1, c2):
        rows = slice(h * HALF, (h + 1) * HALF)

        def prev_rows(u):
            u1 = jnp.where(row == 0, c1, pltpu.roll(u, 1, 0))
            u2 = jnp.where(row == 0, c2, jnp.where(row == 1, c1, pltpu.roll(u, 2, 0)))
            return u1, u2

        u, yc_n, q, k, v, za = yield from _head(x_ref[rows, :], win_ref, nin_ref, cw_ref, ncv_ref, prev_rows)
        kbuf[HIST + h * HALF:HIST + (h + 1) * HALF, :] = k.astype(BF16)
        vbuf[HIST + h * HALF:HIST + (h + 1) * HALF, :] = v.astype(BF16)
        ks_ref[rows, :] = k
        vs_ref[rows, :] = v
        return u, yc_n, q, za

    def attention(h, q, za):
        groups = []
        for gl in range(HALF // GROUP):
            g = h * (HALF // GROUP) + gl
            first_key = t * tile + g * GROUP - HIST
            slabs = []
            for p in range(N_SLABS):
                lanes = slice(p * SLAB_W, (p + 1) * SLAB_W)
                bias = jnp.concatenate(
                    [bias_ref[jnp.where(first_key + cb * LANES < 0, N_SLABS, p), cb] for cb in range(KBLOCKS)],
                    axis=1)
                slabs.append(_attend(q[gl * GROUP:(gl + 1) * GROUP, lanes],
                                     kbuf[g * GROUP:g * GROUP + KWIN, lanes],
                                     vbuf[g * GROUP:g * GROUP + KWIN, lanes], bias))
                yield
            groups.append(jnp.concatenate(slabs, axis=1))
        ya = jnp.concatenate(groups, axis=0) * (za * _sigmoid(za))
        return _rms(ya, nat_ref[...])

    def tail(h, yc_n, ya_n):
        rows = slice(h * HALF, (h + 1) * HALF)
        y_ref[rows, :] = yield from _tail(x_ref[rows, :], yc_n, ya_n, p_ref[rows, :],
                                          wout_ref, wg_ref, wp_ref, npl_ref, nfin_ref)

    u, yc_n, q, za = _run(front(0, ucar[7:8, :], ucar[6:7, :]))
    pending_tail = None
    for h in range(n_half):
        if h + 1 < n_half:
            side = front(h + 1, u[HALF - 1:HALF, :], u[HALF - 2:HALF - 1, :])
        else:
            side = pending_tail
            pending_tail = None
        if pending_tail is not None:
            _run(pending_tail)
        ya_n, nxt = _zip(attention(h, q, za), side) if side is not None else (_run(attention(h, q, za)), None)
        pending_tail = tail(h, yc_n, ya_n)
        if h + 1 < n_half:
            u, yc_n, q, za = nxt
    _run(pending_tail)
    unext[...] = u[HALF - 8:, :]
    cs_ref[...] = u[HALF - 8:, :]


def _sample_kernel(x_ref, p_ref, ck_ref, cv_ref, sc_ref, win_ref, wout_ref, wg_ref, wp_ref, nin_ref,
                   cw_ref, ncv_ref, nat_ref, npl_ref, nfin_ref, gvec_ref,
                   y_ref, ks_ref, vs_ref, cs_ref, kbuf, vbuf, bias_ref):
    nb = ck_ref.shape[0]
    rows = x_ref.shape[0]
    seq = rows // nb

    @pl.when(pl.program_id(0) == 0)
    def _():
        def store(h, tb):
            j = h % SLAB_HEADS
            bias_ref[h // SLAB_HEADS, j * seq:(j + 1) * seq, :] = tb

        _build_bias(gvec_ref, seq, lambda i, s: s < HIST + seq, store)
        kbuf[:, HIST + seq:, :] = jnp.zeros((nb, KWIN - HIST - seq, D_ATT), BF16)
        vbuf[:, HIST + seq:, :] = jnp.zeros((nb, KWIN - HIST - seq, D_ATT), BF16)

    x = x_ref[...]
    rmod = lax.broadcasted_iota(jnp.int32, (rows, D_CONV), 0) % seq

    def prev_rows(u):
        e1 = jnp.concatenate([jnp.broadcast_to(sc_ref[j, 1:2, :], (seq, D_CONV)) for j in range(nb)], axis=0)
        e2 = jnp.concatenate([jnp.broadcast_to(sc_ref[j, 0:1, :], (seq, D_CONV)) for j in range(nb)], axis=0)
        u1 = jnp.where(rmod == 0, e1, pltpu.roll(u, 1, 0))
        u2 = jnp.where(rmod == 0, e2, jnp.where(rmod == 1, e1, pltpu.roll(u, 2, 0)))
        return u1, u2

    u, yc_n, q, k, v, za = _run(_head(x, win_ref, nin_ref, cw_ref, ncv_ref, prev_rows))
    ks_ref[...] = k
    vs_ref[...] = v
    kb = k.astype(BF16)
    vb = v.astype(BF16)

    batches = []
    for j in range(nb):
        rs = slice(j * seq, (j + 1) * seq)
        cs_ref[j] = u[(j + 1) * seq - 8:(j + 1) * seq, :]
        kbuf[j, 0:HIST, :] = ck_ref[j].astype(BF16)
        vbuf[j, 0:HIST, :] = cv_ref[j].astype(BF16)
        kbuf[j, HIST:HIST + seq, :] = kb[rs, :]
        vbuf[j, HIST:HIST + seq, :] = vb[rs, :]
        slabs = []
        for p in range(N_SLABS):
            lanes = slice(p * SLAB_W, (p + 1) * SLAB_W)
            slabs.append(_attend(q[rs, lanes], kbuf[j, :, lanes], vbuf[j, :, lanes], bias_ref[p]))
        batches.append(jnp.concatenate(slabs, axis=1))
    ya = jnp.concatenate(batches, axis=0) * (za * _sigmoid(za))
    ya_n = _rms(ya, nat_ref[...])

    y_ref[...] = _run(_tail(x, yc_n, ya_n, p_ref[...], wout_ref, wg_ref, wp_ref, npl_ref, nfin_ref))


def _bias_vector(rel_bias):
    width = KWIN + GROUP
    flat = KWIN - REL_CLIP
    assert width - flat == 2 * REL_CLIP
    return jnp.concatenate([jnp.broadcast_to(rel_bias[:, 2 * REL_CLIP:], (N_HEADS, flat)),
                            rel_bias[:, ::-1][:, :2 * REL_CLIP]], axis=1)


def _const_spec(shape):
    nd = len(shape)
    return pl.BlockSpec(shape, lambda *_: (0,) * nd, pipeline_mode=pl.Buffered(1))


def kernel(x_prompt, x_sample, cache_k, cache_v, state_conv, p_prompt, p_sample, norm_in, w_in, conv_w,
           rel_bias, norm_conv, norm_att, w_out, ple_norm, w_ple_gate, w_ple_proj, final_norm):
    depth = w_in.shape[0]
    assert depth == 1, "single-layer kernel"
    batch, seq, _ = x_prompt.shape
    dec_batch, dec_seq, _ = x_sample.shape
    kv_win = cache_k.shape[2]
    assert kv_win == HIST and PROMPT_TILE == HIST and seq % PROMPT_TILE == 0
    assert PROMPT_TILE % HALF == 0 and HALF % GROUP == 0 and (1 << HEAD_SHIFT) == HEAD_DIM
    assert dec_seq % 8 == 0 and dec_seq <= GROUP and dec_batch % SAMPLE_BATCH_PER_STEP == 0

    win = w_in[0].astype(BF16)
    wout = w_out[0].astype(BF16)
    wg = w_ple_gate[0].astype(BF16)
    wp = w_ple_proj[0].astype(BF16)
    nin = norm_in[0][None, :]
    ncv = norm_conv[0][None, :]
    nat = norm_att[0][None, :]
    npl = ple_norm[0][None, :]
    nfin = final_norm[None, :]
    cw = conv_w[0]
    gvec = _bias_vector(rel_bias[0])

    weight_specs = [
        _const_spec(win.shape), _const_spec(wout.shape), _const_spec(wg.shape), _const_spec(wp.shape),
        _const_spec(nin.shape), _const_spec(cw.shape), _const_spec(ncv.shape), _const_spec(nat.shape),
        _const_spec(npl.shape), _const_spec(nfin.shape),
    ]
    weights = (win, wout, wg, wp, nin, cw, ncv, nat, npl, nfin)

    n_t = seq // PROMPT_TILE
    row_spec = lambda w: pl.BlockSpec((None, PROMPT_TILE, w), lambda b, t: (b, t, 0))
    state_spec = pl.BlockSpec((None, HIST, D_ATT), lambda b, t: (b, 0, 0))
    y_p, k_p, v_p, c_p = pl.pallas_call(
        _prompt_kernel,
        grid=(batch, n_t),
        in_specs=[row_spec(D_MODEL), row_spec(D_PLE)] + weight_specs + [_const_spec(gvec.shape)],
        out_specs=[row_spec(D_MODEL), state_spec, state_spec,
                   pl.BlockSpec((None, 8, D_CONV), lambda b, t: (b, 0, 0))],
        out_shape=[jax.ShapeDtypeStruct((batch, seq, D_MODEL), F32),
                   jax.ShapeDtypeStruct((batch, HIST, D_ATT), F32),
                   jax.ShapeDtypeStruct((batch, HIST, D_ATT), F32),
                   jax.ShapeDtypeStruct((batch, 8, D_CONV), F32)],
        scratch_shapes=[pltpu.VMEM((HIST + PROMPT_TILE, D_ATT), BF16),
                        pltpu.VMEM((HIST + PROMPT_TILE, D_ATT), BF16),
                        pltpu.VMEM((8, D_CONV), F32),
                        pltpu.VMEM((8, D_CONV), F32),
                        pltpu.VMEM((N_SLABS + 1, KBLOCKS, SLAB_HEADS * GROUP, LANES), F32)],
        compiler_params=pltpu.CompilerParams(dimension_semantics=("arbitrary", "arbitrary"),
                                             vmem_limit_bytes=VMEM_LIMIT_BYTES),
        name="prompt_layer",
    )(x_prompt, p_prompt[0], *weights, gvec)

    nb = SAMPLE_BATCH_PER_STEP
    rows = nb * dec_seq
    flat_spec = lambda w: pl.BlockSpec((rows, w), lambda i: (i, 0))
    y_s, k_s, v_s, c_s = pl.pallas_call(
        _sample_kernel,
        grid=(dec_batch // nb,),
        in_specs=[flat_spec(D_MODEL), flat_spec(D_PLE),
                  pl.BlockSpec((nb, HIST, D_ATT), lambda i: (i, 0, 0)),
                  pl.BlockSpec((nb, HIST, D_ATT), lambda i: (i, 0, 0)),
                  pl.BlockSpec((nb, CONV_W - 1, D_CONV), lambda i: (i, 0, 0))]
                 + weight_specs + [_const_spec(gvec.shape)],
        out_specs=[flat_spec(D_MODEL), flat_spec(D_ATT), flat_spec(D_ATT),
                   pl.BlockSpec((nb, 8, D_CONV), lambda i: (i, 0, 0))],
        out_shape=[jax.ShapeDtypeStruct((dec_batch * dec_seq, D_MODEL), F32),
                   jax.ShapeDtypeStruct((dec_batch * dec_seq, D_ATT), F32),
                   jax.ShapeDtypeStruct((dec_batch * dec_seq, D_ATT), F32),
                   jax.ShapeDtypeStruct((dec_batch, 8, D_CONV), F32)],
        scratch_shapes=[pltpu.VMEM((nb, KWIN, D_ATT), BF16), pltpu.VMEM((nb, KWIN, D_ATT), BF16),
                        pltpu.VMEM((N_SLABS, SLAB_HEADS * dec_seq, KWIN), F32)],
        compiler_params=pltpu.CompilerParams(dimension_semantics=("arbitrary",),
                                             vmem_limit_bytes=VMEM_LIMIT_BYTES),
        name="sample_layer",
    )(x_sample.reshape(dec_batch * dec_seq, D_MODEL), p_sample[0].reshape(dec_batch * dec_seq, D_PLE),
      cache_k[0].reshape(dec_batch, HIST, D_ATT), cache_v[0].reshape(dec_batch, HIST, D_ATT),
      state_conv[0], *weights, gvec)

    return (y_p,
            y_s.reshape(dec_batch, dec_seq, D_MODEL),
            k_p.reshape(1, batch, HIST, N_HEADS, HEAD_DIM),
            v_p.reshape(1, batch, HIST, N_HEADS, HEAD_DIM),
            c_p[None, :, 8 - (CONV_W - 1):, :],
            k_s.reshape(1, dec_batch, dec_seq, N_HEADS, HEAD_DIM),
            v_s.reshape(1, dec_batch, dec_seq, N_HEADS, HEAD_DIM),
            c_s[None, :, 8 - (CONV_W - 1):, :])
```

```python
import jax
import jax.numpy as jnp
from jax import lax
from jax.experimental import pallas as pl
from jax.experimental.pallas import tpu as pltpu

D_MODEL = 1024
D_CONV = 512
D_ATT = 512
HEAD_DIM = 64
HEAD_SHIFT = 6
N_HEADS = 8
SLAB_HEADS = 4
N_SLABS = N_HEADS // SLAB_HEADS
LANES = 128
SLAB_W = SLAB_HEADS * HEAD_DIM
CHUNK = 64
N_PAST_CHUNKS = 8
HIST = N_PAST_CHUNKS * CHUNK
REL_CLIP = 128
D_PLE = 256
CONV_W = 3
EPS = 1e-6
NEG = -1e30
SCALE = HEAD_DIM ** -0.5

GROUP = 2 * CHUNK
KWIN = HIST + GROUP
KBLOCKS = KWIN // LANES
PROMPT_TILE = 512
HALF = 256
SAMPLE_BATCH_PER_STEP = 4
VMEM_LIMIT_BYTES = 56 * 1024 * 1024

F32 = jnp.float32
BF16 = jnp.bfloat16


def _rms(x, g):
    ms = jnp.mean(x * x, axis=-1, keepdims=True)
    return x * lax.rsqrt(ms + EPS) * g


def _sigmoid(x):
    return 1.0 / (1.0 + jnp.exp(-x))


def _dot(a, b):
    return jnp.dot(a, b, preferred_element_type=F32)


def _run(steps):
    try:
        while True:
            next(steps)
    except StopIteration as done:
        return done.value


def _chain(*stages):
    results = []
    for steps in stages:
        results.append((yield from steps))
    return results


def _zip(main, side, side_per_main):
    side_val, side_done = None, False
    while True:
        try:
            next(main)
        except StopIteration as done:
            main_val = done.value
            break
        for _ in range(side_per_main):
            if not side_done:
                try:
                    next(side)
                except StopIteration as done:
                    side_val, side_done = done.value, True
    if not side_done:
        side_val = _run(side)
    return main_val, side_val


def _proj(xn, win_ref, i):
    return _dot(xn, win_ref[:, 512 * i:512 * (i + 1)])


def _qkv(x, win_ref, nin_ref):
    xn = _rms(x, nin_ref[...]).astype(BF16)
    q = (_proj(xn, win_ref, 4) * SCALE).astype(BF16)
    yield
    k = _proj(xn, win_ref, 5)
    yield
    v = _proj(xn, win_ref, 6)
    return xn, q, k, v


def _conv_gate(xn, win_ref, cw_ref, ncv_ref, prev_rows):
    h = _proj(xn, win_ref, 0)
    yield
    u = _proj(xn, win_ref, 2) * h
    u1, u2 = prev_rows(u)
    cv = cw_ref[0:1, :] * u2 + cw_ref[1:2, :] * u1 + cw_ref[2:3, :] * u
    yield
    zc = _proj(xn, win_ref, 3)
    gz = zc * _sigmoid(zc)
    yield
    yc = _proj(xn, win_ref, 1) * cv * gz
    yc_n = _rms(yc, ncv_ref[...])
    yield
    za = _proj(xn, win_ref, 7)
    return u, yc_n, za


def _attend(qslab, kw, vw, bias):
    r = qslab.shape[0]
    lane = lax.broadcasted_iota(jnp.int32, qslab.shape, 1) >> HEAD_SHIFT
    zero = jnp.zeros_like(qslab)
    qs = jnp.concatenate([jnp.where(lane == j, qslab, zero) for j in range(SLAB_HEADS)], axis=0)
    s = lax.dot_general(qs, kw, (((1,), (1,)), ((), ())), preferred_element_type=F32)
    s = s + bias
    m = jnp.max(s, axis=-1, keepdims=True)
    e = jnp.exp(s - m)
    l = jnp.sum(e, axis=-1, keepdims=True)
    o = _dot(e.astype(BF16), vw) * (1.0 / l)
    out = o[(SLAB_HEADS - 1) * r:]
    for j in reversed(range(SLAB_HEADS - 1)):
        out = jnp.where(lane <= j, o[j * r:(j + 1) * r], out)
    return out


def _build_bias(gvec_ref, n_rows, in_band, store):
    width = gvec_ref.shape[1]
    i = lax.broadcasted_iota(jnp.int32, (n_rows, width), 0)
    band = in_band(lax.broadcasted_iota(jnp.int32, (n_rows, KWIN), 0),
                   lax.broadcasted_iota(jnp.int32, (n_rows, KWIN), 1))
    for h in range(N_HEADS):
        x = jnp.broadcast_to(gvec_ref[h:h + 1, :], (n_rows, width))
        bit = 0
        while (1 << bit) < n_rows:
            x = jnp.where(((i >> bit) & 1) == 1, pltpu.roll(x, 1 << bit, 1), x)
            bit += 1
        store(h, jnp.where(band, x[:, GROUP:GROUP + KWIN], NEG))


def _tail(x, yc_n, ya, za, p, wout_ref, wg_ref, wp_ref, nat_ref, npl_ref, nfin_ref):
    pe = _dot(p.astype(BF16), wp_ref[...])
    ya_n = _rms(ya * (za * _sigmoid(za)), nat_ref[...])
    yield
    y = _dot(yc_n.astype(BF16), wout_ref[0:D_CONV, :]) + _dot(ya_n.astype(BF16), wout_ref[D_CONV:, :])
    x1 = x + y
    yield
    gate = _sigmoid(_dot(_rms(x1, npl_ref[...]).astype(BF16), wg_ref[...]))
    x2 = x1 + gate * pe
    return _rms(x2, nfin_ref[...])


def _prompt_kernel(x_ref, p_ref, win_ref, wout_ref, wg_ref, wp_ref, nin_ref, cw_ref, ncv_ref,
                   nat_ref, npl_ref, nfin_ref, gvec_ref,
                   y_ref, ks_ref, vs_ref, cs_ref, kbuf, vbuf, ucar, unext, bias_ref):
    t = pl.program_id(1)
    tile = x_ref.shape[0]

    @pl.when((pl.program_id(0) == 0) & (t == 0))
    def _():
        def store(h, tb):
            j = h % SLAB_HEADS
            for cb in range(KBLOCKS):
                bias_ref[h // SLAB_HEADS, cb, j * GROUP:(j + 1) * GROUP, :] = tb[:, cb * LANES:(cb + 1) * LANES]

        _build_bias(gvec_ref, GROUP,
                    lambda i, s: ((i < CHUNK) & (s < HIST + CHUNK)) | ((i >= CHUNK) & (s >= CHUNK)), store)
        bias_ref[N_SLABS] = jnp.full((KBLOCKS, SLAB_HEADS * GROUP, LANES), NEG, F32)

    @pl.when(t == 0)
    def _():
        kbuf[0:HIST, :] = jnp.zeros((HIST, D_ATT), BF16)
        vbuf[0:HIST, :] = jnp.zeros((HIST, D_ATT), BF16)
        ucar[...] = jnp.zeros(ucar.shape, F32)

    @pl.when(t > 0)
    def _():
        kbuf[0:HIST, :] = kbuf[tile:, :]
        vbuf[0:HIST, :] = vbuf[tile:, :]
        ucar[...] = unext[...]

    n_half = tile // HALF
    row = lax.broadcasted_iota(jnp.int32, (HALF, D_CONV), 0)

    def qkv(h):
        rows = slice(h * HALF, (h + 1) * HALF)
        xn, q, k, v = yield from _qkv(x_ref[rows, :], win_ref, nin_ref)
        kbuf[HIST + h * HALF:HIST + (h + 1) * HALF, :] = k.astype(BF16)
        vbuf[HIST + h * HALF:HIST + (h + 1) * HALF, :] = v.astype(BF16)
        ks_ref[rows, :] = k
        vs_ref[rows, :] = v
        return xn, q

    def conv_gate(xn, c1, c2):
        def prev_rows(u):
            u1 = jnp.where(row == 0, c1, pltpu.roll(u, 1, 0))
            u2 = jnp.where(row == 0, c2, jnp.where(row == 1, c1, pltpu.roll(u, 2, 0)))
            return u1, u2

        return _conv_gate(xn, win_ref, cw_ref, ncv_ref, prev_rows)

    def attention(h, q):
        groups = []
        for gl in range(HALF // GROUP):
            g = h * (HALF // GROUP) + gl
            first_key = t * tile + g * GROUP - HIST
            slabs = []
            for p in range(N_SLABS):
                lanes = slice(p * SLAB_W, (p + 1) * SLAB_W)
                bias = jnp.concatenate(
                    [bias_ref[jnp.where(first_key + cb * LANES < 0, N_SLABS, p), cb] for cb in range(KBLOCKS)],
                    axis=1)
                slabs.append(_attend(q[gl * GROUP:(gl + 1) * GROUP, lanes],
                                     kbuf[g * GROUP:g * GROUP + KWIN, lanes],
                                     vbuf[g * GROUP:g * GROUP + KWIN, lanes], bias))
                yield
            groups.append(jnp.concatenate(slabs, axis=1))
        return jnp.concatenate(groups, axis=0)

    def tail(h, yc_n, ya, za):
        rows = slice(h * HALF, (h + 1) * HALF)
        y_ref[rows, :] = yield from _tail(x_ref[rows, :], yc_n, ya, za, p_ref[rows, :],
                                          wout_ref, wg_ref, wp_ref, nat_ref, npl_ref, nfin_ref)

    xn, q = _run(qkv(0))
    c1, c2 = ucar[7:8, :], ucar[6:7, :]
    done = None
    for h in range(n_half):
        side = [conv_gate(xn, c1, c2)]
        if h + 1 < n_half:
            side.append(qkv(h + 1))
        if done is not None:
            side.append(tail(h - 1, *done))
        ya, results = _zip(attention(h, q), _chain(*side), 2)
        u, yc_n, za = results[0]
        done = (yc_n, ya, za)
        c1, c2 = u[HALF - 1:HALF, :], u[HALF - 2:HALF - 1, :]
        if h + 1 < n_half:
            xn, q = results[1]
    _run(tail(n_half - 1, *done))
    unext[...] = u[HALF - 8:, :]
    cs_ref[...] = u[HALF - 8:, :]


def _sample_kernel(x_ref, p_ref, ck_ref, cv_ref, sc_ref, win_ref, wout_ref, wg_ref, wp_ref, nin_ref,
                   cw_ref, ncv_ref, nat_ref, npl_ref, nfin_ref, gvec_ref,
                   y_ref, ks_ref, vs_ref, cs_ref, kbuf, vbuf, bias_ref):
    nb = ck_ref.shape[0]
    rows = x_ref.shape[0]
    seq = rows // nb

    @pl.when(pl.program_id(0) == 0)
    def _():
        def store(h, tb):
            j = h % SLAB_HEADS
            bias_ref[h // SLAB_HEADS, j * seq:(j + 1) * seq, :] = tb

        _build_bias(gvec_ref, seq, lambda i, s: s < HIST + seq, store)
        kbuf[:, HIST + seq:, :] = jnp.zeros((nb, KWIN - HIST - seq, D_ATT), BF16)
        vbuf[:, HIST + seq:, :] = jnp.zeros((nb, KWIN - HIST - seq, D_ATT), BF16)

    x = x_ref[...]
    rmod = lax.broadcasted_iota(jnp.int32, (rows, D_CONV), 0) % seq

    def prev_rows(u):
        e1 = jnp.concatenate([jnp.broadcast_to(sc_ref[j, 1:2, :], (seq, D_CONV)) for j in range(nb)], axis=0)
        e2 = jnp.concatenate([jnp.broadcast_to(sc_ref[j, 0:1, :], (seq, D_CONV)) for j in range(nb)], axis=0)
        u1 = jnp.where(rmod == 0, e1, pltpu.roll(u, 1, 0))
        u2 = jnp.where(rmod == 0, e2, jnp.where(rmod == 1, e1, pltpu.roll(u, 2, 0)))
        return u1, u2

    xn, q, k, v = _run(_qkv(x, win_ref, nin_ref))
    u, yc_n, za = _run(_conv_gate(xn, win_ref, cw_ref, ncv_ref, prev_rows))
    ks_ref[...] = k
    vs_ref[...] = v
    kb = k.astype(BF16)
    vb = v.astype(BF16)

    batches = []
    for j in range(nb):
        rs = slice(j * seq, (j + 1) * seq)
        cs_ref[j] = u[(j + 1) * seq - 8:(j + 1) * seq, :]
        kbuf[j, 0:HIST, :] = ck_ref[j].astype(BF16)
        vbuf[j, 0:HIST, :] = cv_ref[j].astype(BF16)
        kbuf[j, HIST:HIST + seq, :] = kb[rs, :]
        vbuf[j, HIST:HIST + seq, :] = vb[rs, :]
        slabs = []
        for p in range(N_SLABS):
            lanes = slice(p * SLAB_W, (p + 1) * SLAB_W)
            slabs.append(_attend(q[rs, lanes], kbuf[j, :, lanes], vbuf[j, :, lanes], bias_ref[p]))
        batches.append(jnp.concatenate(slabs, axis=1))
    ya = jnp.concatenate(batches, axis=0)
    y_ref[...] = _run(_tail(x, yc_n, ya, za, p_ref[...], wout_ref, wg_ref, wp_ref, nat_ref, npl_ref, nfin_ref))


def _bias_vector(rel_bias):
    width = KWIN + GROUP
    flat = KWIN - REL_CLIP
    assert width - flat == 2 * REL_CLIP
    return jnp.concatenate([jnp.broadcast_to(rel_bias[:, 2 * REL_CLIP:], (N_HEADS, flat)),
                            rel_bias[:, ::-1][:, :2 * REL_CLIP]], axis=1)


def _const_spec(shape):
    nd = len(shape)
    return pl.BlockSpec(shape, lambda *_: (0,) * nd, pipeline_mode=pl.Buffered(1))


def kernel(x_prompt, x_sample, cache_k, cache_v, state_conv, p_prompt, p_sample, norm_in, w_in, conv_w,
           rel_bias, norm_conv, norm_att, w_out, ple_norm, w_ple_gate, w_ple_proj, final_norm):
    depth = w_in.shape[0]
    assert depth == 1, "single-layer kernel"
    batch, seq, _ = x_prompt.shape
    dec_batch, dec_seq, _ = x_sample.shape
    kv_win = cache_k.shape[2]
    assert kv_win == HIST and PROMPT_TILE == HIST and seq % PROMPT_TILE == 0
    assert PROMPT_TILE % HALF == 0 and HALF % GROUP == 0 and (1 << HEAD_SHIFT) == HEAD_DIM
    assert dec_seq % 8 == 0 and dec_seq <= GROUP and dec_batch % SAMPLE_BATCH_PER_STEP == 0

    win = w_in[0].astype(BF16)
    wout = w_out[0].astype(BF16)
    wg = w_ple_gate[0].astype(BF16)
    wp = w_ple_proj[0].astype(BF16)
    nin = norm_in[0][None, :]
    ncv = norm_conv[0][None, :]
    nat = norm_att[0][None, :]
    npl = ple_norm[0][None, :]
    nfin = final_norm[None, :]
    cw = conv_w[0]
    gvec = _bias_vector(rel_bias[0])

    weight_specs = [
        _const_spec(win.shape), _const_spec(wout.shape), _const_spec(wg.shape), _const_spec(wp.shape),
        _const_spec(nin.shape), _const_spec(cw.shape), _const_spec(ncv.shape), _const_spec(nat.shape),
        _const_spec(npl.shape), _const_spec(nfin.shape),
    ]
    weights = (win, wout, wg, wp, nin, cw, ncv, nat, npl, nfin)

    n_t = seq // PROMPT_TILE
    row_spec = lambda w: pl.BlockSpec((None, PROMPT_TILE, w), lambda b, t: (b, t, 0))
    state_spec = pl.BlockSpec((None, HIST, D_ATT), lambda b, t: (b, 0, 0))
    y_p, k_p, v_p, c_p = pl.pallas_call(
        _prompt_kernel,
        grid=(batch, n_t),
        in_specs=[row_spec(D_MODEL), row_spec(D_PLE)] + weight_specs + [_const_spec(gvec.shape)],
        out_specs=[row_spec(D_MODEL), state_spec, state_spec,
                   pl.BlockSpec((None, 8, D_CONV), lambda b, t: (b, 0, 0))],
        out_shape=[jax.ShapeDtypeStruct((batch, seq, D_MODEL), F32),
                   jax.ShapeDtypeStruct((batch, HIST, D_ATT), F32),
                   jax.ShapeDtypeStruct((batch, HIST, D_ATT), F32),
                   jax.ShapeDtypeStruct((batch, 8, D_CONV), F32)],
        scratch_shapes=[pltpu.VMEM((HIST + PROMPT_TILE, D_ATT), BF16),
                        pltpu.VMEM((HIST + PROMPT_TILE, D_ATT), BF16),
                        pltpu.VMEM((8, D_CONV), F32),
                        pltpu.VMEM((8, D_CONV), F32),
                        pltpu.VMEM((N_SLABS + 1, KBLOCKS, SLAB_HEADS * GROUP, LANES), F32)],
        compiler_params=pltpu.CompilerParams(dimension_semantics=("arbitrary", "arbitrary"),
                                             vmem_limit_bytes=VMEM_LIMIT_BYTES),
        name="prompt_layer",
    )(x_prompt, p_prompt[0], *weights, gvec)

    nb = SAMPLE_BATCH_PER_STEP
    rows = nb * dec_seq
    flat_spec = lambda w: pl.BlockSpec((rows, w), lambda i: (i, 0))
    y_s, k_s, v_s, c_s = pl.pallas_call(
        _sample_kernel,
        grid=(dec_batch // nb,),
        in_specs=[flat_spec(D_MODEL), flat_spec(D_PLE),
                  pl.BlockSpec((nb, HIST, D_ATT), lambda i: (i, 0, 0)),
                  pl.BlockSpec((nb, HIST, D_ATT), lambda i: (i, 0, 0)),
                  pl.BlockSpec((nb, CONV_W - 1, D_CONV), lambda i: (i, 0, 0))]
                 + weight_specs + [_const_spec(gvec.shape)],
        out_specs=[flat_spec(D_MODEL), flat_spec(D_ATT), flat_spec(D_ATT),
                   pl.BlockSpec((nb, 8, D_CONV), lambda i: (i, 0, 0))],
        out_shape=[jax.ShapeDtypeStruct((dec_batch * dec_seq, D_MODEL), F32),
                   jax.ShapeDtypeStruct((dec_batch * dec_seq, D_ATT), F32),
                   jax.ShapeDtypeStruct((dec_batch * dec_seq, D_ATT), F32),
                   jax.ShapeDtypeStruct((dec_batch, 8, D_CONV), F32)],
        scratch_shapes=[pltpu.VMEM((nb, KWIN, D_ATT), BF16), pltpu.VMEM((nb, KWIN, D_ATT), BF16),
                        pltpu.VMEM((N_SLABS, SLAB_HEADS * dec_seq, KWIN), F32)],
        compiler_params=pltpu.CompilerParams(dimension_semantics=("arbitrary",),
                                             vmem_limit_bytes=VMEM_LIMIT_BYTES),
        name="sample_layer",
    )(x_sample.reshape(dec_batch * dec_seq, D_MODEL), p_sample[0].reshape(dec_batch * dec_seq, D_PLE),
      cache_k[0].reshape(dec_batch, HIST, D_ATT), cache_v[0].reshape(dec_batch, HIST, D_ATT),
      state_conv[0], *weights, gvec)

    return (y_p,
            y_s.reshape(dec_batch, dec_seq, D_MODEL),
            k_p.reshape(1, batch, HIST, N_HEADS, HEAD_DIM),
            v_p.reshape(1, batch, HIST, N_HEADS, HEAD_DIM),
            c_p[None, :, 8 - (CONV_W - 1):, :],
            k_s.reshape(1, dec_batch, dec_seq, N_HEADS, HEAD_DIM),
            v_s.reshape(1, dec_batch, dec_seq, N_HEADS, HEAD_DIM),
            c_s[None, :, 8 - (CONV_W - 1):, :])
```

```python
import jax
import jax.numpy as jnp
from jax import lax
from jax.experimental import pallas as pl
from jax.experimental.pallas import tpu as pltpu

D_MODEL = 1024
D_CONV = 512
D_ATT = 512
HEAD_DIM = 64
HEAD_SHIFT = 6
N_HEADS = 8
SLAB_HEADS = 4
N_SLABS = N_HEADS // SLAB_HEADS
LANES = 128
SLAB_W = SLAB_HEADS * HEAD_DIM
CHUNK = 64
N_PAST_CHUNKS = 8
HIST = N_PAST_CHUNKS * CHUNK
REL_CLIP = 128
D_PLE = 256
CONV_W = 3
EPS = 1e-6
NEG = -1e30
SCALE = HEAD_DIM ** -0.5

GROUP = 2 * CHUNK
KWIN = HIST + GROUP
KBLOCKS = KWIN // LANES
PROMPT_TILE = 512
HALF = 256
SAMPLE_BATCH_PER_STEP = 8
VMEM_LIMIT_BYTES = 56 * 1024 * 1024

F32 = jnp.float32
BF16 = jnp.bfloat16


def _rms(x, g):
    ms = jnp.mean(x * x, axis=-1, keepdims=True)
    return x * lax.rsqrt(ms + EPS) * g


def _sigmoid(x):
    return 1.0 / (1.0 + jnp.exp(-x))


def _dot(a, b):
    return jnp.dot(a, b, preferred_element_type=F32)


def _run(steps):
    try:
        while True:
            next(steps)
    except StopIteration as done:
        return done.value


def _chain(*stages):
    results = []
    for steps in stages:
        results.append((yield from steps))
    return results


def _zip(main, side, side_per_main):
    side_val, side_done = None, False
    while True:
        try:
            next(main)
        except StopIteration as done:
            main_val = done.value
            break
        for _ in range(side_per_main):
            if not side_done:
                try:
                    next(side)
                except StopIteration as done:
                    side_val, side_done = done.value, True
    if not side_done:
        side_val = _run(side)
    return main_val, side_val


def _proj(xn, win_ref, i):
    return _dot(xn, win_ref[:, 512 * i:512 * (i + 1)])


def _qkv(x, win_ref, nin_ref):
    xn = _rms(x, nin_ref[...]).astype(BF16)
    q = (_proj(xn, win_ref, 4) * SCALE).astype(BF16)
    yield
    k = _proj(xn, win_ref, 5)
    yield
    v = _proj(xn, win_ref, 6)
    return xn, q, k, v


def _conv_gate(xn, win_ref, cw_ref, ncv_ref, prev_rows):
    h = _proj(xn, win_ref, 0)
    yield
    u = _proj(xn, win_ref, 2) * h
    u1, u2 = prev_rows(u)
    cv = cw_ref[0:1, :] * u2 + cw_ref[1:2, :] * u1 + cw_ref[2:3, :] * u
    yield
    zc = _proj(xn, win_ref, 3)
    gz = zc * _sigmoid(zc)
    yield
    yc = _proj(xn, win_ref, 1) * cv * gz
    yc_n = _rms(yc, ncv_ref[...])
    yield
    za = _proj(xn, win_ref, 7)
    return u, yc_n, za


def _attend(qslab, kw, vw, bias):
    r = qslab.shape[0]
    lane = lax.broadcasted_iota(jnp.int32, qslab.shape, 1) >> HEAD_SHIFT
    zero = jnp.zeros_like(qslab)
    qs = jnp.concatenate([jnp.where(lane == j, qslab, zero) for j in range(SLAB_HEADS)], axis=0)
    s = lax.dot_general(qs, kw, (((1,), (1,)), ((), ())), preferred_element_type=F32)
    s = s + bias
    m = jnp.max(s, axis=-1, keepdims=True)
    e = jnp.exp(s - m)
    l = jnp.sum(e, axis=-1, keepdims=True)
    o = _dot(e.astype(BF16), vw) * (1.0 / l)
    out = o[(SLAB_HEADS - 1) * r:]
    for j in reversed(range(SLAB_HEADS - 1)):
        out = jnp.where(lane <= j, o[j * r:(j + 1) * r], out)
    return out


def _build_bias(gvec_ref, n_rows, in_band, store):
    width = gvec_ref.shape[1]
    i = lax.broadcasted_iota(jnp.int32, (n_rows, width), 0)
    band = in_band(lax.broadcasted_iota(jnp.int32, (n_rows, KWIN), 0),
                   lax.broadcasted_iota(jnp.int32, (n_rows, KWIN), 1))
    for h in range(N_HEADS):
        x = jnp.broadcast_to(gvec_ref[h:h + 1, :], (n_rows, width))
        bit = 0
        while (1 << bit) < n_rows:
            x = jnp.where(((i >> bit) & 1) == 1, pltpu.roll(x, 1 << bit, 1), x)
            bit += 1
        store(h, jnp.where(band, x[:, GROUP:GROUP + KWIN], NEG))


def _tail(x, yc_n, ya, za, p, wout_ref, wg_ref, wp_ref, nat_ref, npl_ref, nfin_ref):
    pe = _dot(p.astype(BF16), wp_ref[...])
    ya_n = _rms(ya * (za * _sigmoid(za)), nat_ref[...])
    yield
    y = _dot(yc_n.astype(BF16), wout_ref[0:D_CONV, :]) + _dot(ya_n.astype(BF16), wout_ref[D_CONV:, :])
    x1 = x + y
    yield
    gate = _sigmoid(_dot(_rms(x1, npl_ref[...]).astype(BF16), wg_ref[...]))
    x2 = x1 + gate * pe
    return _rms(x2, nfin_ref[...])


def _prompt_kernel(x_ref, p_ref, win_ref, wout_ref, wg_ref, wp_ref, nin_ref, cw_ref, ncv_ref,
                   nat_ref, npl_ref, nfin_ref, gvec_ref,
                   y_ref, ks_ref, vs_ref, cs_ref, kbuf, vbuf, ucar, unext, bias_ref):
    t = pl.program_id(1)
    tile = x_ref.shape[0]

    @pl.when((pl.program_id(0) == 0) & (t == 0))
    def _():
        def store(h, tb):
            j = h % SLAB_HEADS
            for cb in range(KBLOCKS):
                bias_ref[h // SLAB_HEADS, cb, j * GROUP:(j + 1) * GROUP, :] = tb[:, cb * LANES:(cb + 1) * LANES]

        _build_bias(gvec_ref, GROUP,
                    lambda i, s: ((i < CHUNK) & (s < HIST + CHUNK)) | ((i >= CHUNK) & (s >= CHUNK)), store)
        bias_ref[N_SLABS] = jnp.full((KBLOCKS, SLAB_HEADS * GROUP, LANES), NEG, F32)

    @pl.when(t == 0)
    def _():
        kbuf[0:HIST, :] = jnp.zeros((HIST, D_ATT), BF16)
        vbuf[0:HIST, :] = jnp.zeros((HIST, D_ATT), BF16)
        ucar[...] = jnp.zeros(ucar.shape, F32)

    @pl.when(t > 0)
    def _():
        kbuf[0:HIST, :] = kbuf[tile:, :]
        vbuf[0:HIST, :] = vbuf[tile:, :]
        ucar[...] = unext[...]

    n_half = tile // HALF
    row = lax.broadcasted_iota(jnp.int32, (HALF, D_CONV), 0)

    def qkv(h):
        rows = slice(h * HALF, (h + 1) * HALF)
        xn, q, k, v = yield from _qkv(x_ref[rows, :], win_ref, nin_ref)
        kbuf[HIST + h * HALF:HIST + (h + 1) * HALF, :] = k.astype(BF16)
        vbuf[HIST + h * HALF:HIST + (h + 1) * HALF, :] = v.astype(BF16)
        ks_ref[rows, :] = k
        vs_ref[rows, :] = v
        return xn, q

    def conv_gate(xn, c1, c2):
        def prev_rows(u):
            u1 = jnp.where(row == 0, c1, pltpu.roll(u, 1, 0))
            u2 = jnp.where(row == 0, c2, jnp.where(row == 1, c1, pltpu.roll(u, 2, 0)))
            return u1, u2

        return _conv_gate(xn, win_ref, cw_ref, ncv_ref, prev_rows)

    def attention(h, q):
        groups = []
        for gl in range(HALF // GROUP):
            g = h * (HALF // GROUP) + gl
            first_key = t * tile + g * GROUP - HIST
            slabs = []
            for p in range(N_SLABS):
                lanes = slice(p * SLAB_W, (p + 1) * SLAB_W)
                bias = jnp.concatenate(
                    [bias_ref[jnp.where(first_key + cb * LANES < 0, N_SLABS, p), cb] for cb in range(KBLOCKS)],
                    axis=1)
                slabs.append(_attend(q[gl * GROUP:(gl + 1) * GROUP, lanes],
                                     kbuf[g * GROUP:g * GROUP + KWIN, lanes],
                                     vbuf[g * GROUP:g * GROUP + KWIN, lanes], bias))
                yield
            groups.append(jnp.concatenate(slabs, axis=1))
        return jnp.concatenate(groups, axis=0)

    def tail(h, yc_n, ya, za):
        rows = slice(h * HALF, (h + 1) * HALF)
        y_ref[rows, :] = yield from _tail(x_ref[rows, :], yc_n, ya, za, p_ref[rows, :],
                                          wout_ref, wg_ref, wp_ref, nat_ref, npl_ref, nfin_ref)

    xn, q = _run(qkv(0))
    c1, c2 = ucar[7:8, :], ucar[6:7, :]
    done = None
    for h in range(n_half):
        side = [conv_gate(xn, c1, c2)]
        if h + 1 < n_half:
            side.append(qkv(h + 1))
        if done is not None:
            side.append(tail(h - 1, *done))
        ya, results = _zip(attention(h, q), _chain(*side), 2)
        u, yc_n, za = results[0]
        done = (yc_n, ya, za)
        c1, c2 = u[HALF - 1:HALF, :], u[HALF - 2:HALF - 1, :]
        if h + 1 < n_half:
            xn, q = results[1]
    _run(tail(n_half - 1, *done))
    unext[...] = u[HALF - 8:, :]
    cs_ref[...] = u[HALF - 8:, :]


def _sample_kernel(x_ref, p_ref, ck_ref, cv_ref, sc_ref, win_ref, wout_ref, wg_ref, wp_ref, nin_ref,
                   cw_ref, ncv_ref, nat_ref, npl_ref, nfin_ref, gvec_ref,
                   y_ref, ks_ref, vs_ref, cs_ref, kbuf, vbuf, bias_ref):
    nb = ck_ref.shape[0]
    rows = x_ref.shape[0]
    seq = rows // nb

    @pl.when(pl.program_id(0) == 0)
    def _():
        def store(h, tb):
            j = h % SLAB_HEADS
            bias_ref[h // SLAB_HEADS, j * seq:(j + 1) * seq, :] = tb

        _build_bias(gvec_ref, seq, lambda i, s: s < HIST + seq, store)
        kbuf[:, HIST + seq:, :] = jnp.zeros((nb, KWIN - HIST - seq, D_ATT), BF16)
        vbuf[:, HIST + seq:, :] = jnp.zeros((nb, KWIN - HIST - seq, D_ATT), BF16)

    x = x_ref[...]
    rmod = lax.broadcasted_iota(jnp.int32, (rows, D_CONV), 0) % seq

    def prev_rows(u):
        e1 = jnp.concatenate([jnp.broadcast_to(sc_ref[j, 1:2, :], (seq, D_CONV)) for j in range(nb)], axis=0)
        e2 = jnp.concatenate([jnp.broadcast_to(sc_ref[j, 0:1, :], (seq, D_CONV)) for j in range(nb)], axis=0)
        u1 = jnp.where(rmod == 0, e1, pltpu.roll(u, 1, 0))
        u2 = jnp.where(rmod == 0, e2, jnp.where(rmod == 1, e1, pltpu.roll(u, 2, 0)))
        return u1, u2

    xn, q, k, v = _run(_qkv(x, win_ref, nin_ref))
    u, yc_n, za = _run(_conv_gate(xn, win_ref, cw_ref, ncv_ref, prev_rows))
    ks_ref[...] = k
    vs_ref[...] = v
    kb = k.astype(BF16)
    vb = v.astype(BF16)

    batches = []
    for j in range(nb):
        rs = slice(j * seq, (j + 1) * seq)
        cs_ref[j] = u[(j + 1) * seq - 8:(j + 1) * seq, :]
        kbuf[j, 0:HIST, :] = ck_ref[j]
        vbuf[j, 0:HIST, :] = cv_ref[j]
        kbuf[j, HIST:HIST + seq, :] = kb[rs, :]
        vbuf[j, HIST:HIST + seq, :] = vb[rs, :]
        slabs = []
        for p in range(N_SLABS):
            lanes = slice(p * SLAB_W, (p + 1) * SLAB_W)
            slabs.append(_attend(q[rs, lanes], kbuf[j, :, lanes], vbuf[j, :, lanes], bias_ref[p]))
        batches.append(jnp.concatenate(slabs, axis=1))
    ya = jnp.concatenate(batches, axis=0)
    y_ref[...] = _run(_tail(x, yc_n, ya, za, p_ref[...], wout_ref, wg_ref, wp_ref, nat_ref, npl_ref, nfin_ref))


def _bias_vector(rel_bias):
    width = KWIN + GROUP
    flat = KWIN - REL_CLIP
    assert width - flat == 2 * REL_CLIP
    return jnp.concatenate([jnp.broadcast_to(rel_bias[:, 2 * REL_CLIP:], (N_HEADS, flat)),
                            rel_bias[:, ::-1][:, :2 * REL_CLIP]], axis=1)


def _const_spec(shape):
    nd = len(shape)
    return pl.BlockSpec(shape, lambda *_: (0,) * nd, pipeline_mode=pl.Buffered(1))


def kernel(x_prompt, x_sample, cache_k, cache_v, state_conv, p_prompt, p_sample, norm_in, w_in, conv_w,
           rel_bias, norm_conv, norm_att, w_out, ple_norm, w_ple_gate, w_ple_proj, final_norm):
    depth = w_in.shape[0]
    assert depth == 1, "single-layer kernel"
    batch, seq, _ = x_prompt.shape
    dec_batch, dec_seq, _ = x_sample.shape
    kv_win = cache_k.shape[2]
    assert kv_win == HIST and PROMPT_TILE == HIST and seq % PROMPT_TILE == 0
    assert PROMPT_TILE % HALF == 0 and HALF % GROUP == 0 and (1 << HEAD_SHIFT) == HEAD_DIM
    assert dec_seq % 8 == 0 and dec_seq <= GROUP and dec_batch % SAMPLE_BATCH_PER_STEP == 0

    win = w_in[0].astype(BF16)
    wout = w_out[0].astype(BF16)
    wg = w_ple_gate[0].astype(BF16)
    wp = w_ple_proj[0].astype(BF16)
    nin = norm_in[0][None, :]
    ncv = norm_conv[0][None, :]
    nat = norm_att[0][None, :]
    npl = ple_norm[0][None, :]
    nfin = final_norm[None, :]
    cw = conv_w[0]
    gvec = _bias_vector(rel_bias[0])

    weight_specs = [
        _const_spec(win.shape), _const_spec(wout.shape), _const_spec(wg.shape), _const_spec(wp.shape),
        _const_spec(nin.shape), _const_spec(cw.shape), _const_spec(ncv.shape), _const_spec(nat.shape),
        _const_spec(npl.shape), _const_spec(nfin.shape),
    ]
    weights = (win, wout, wg, wp, nin, cw, ncv, nat, npl, nfin)

    n_t = seq // PROMPT_TILE
    row_spec = lambda w: pl.BlockSpec((None, PROMPT_TILE, w), lambda b, t: (b, t, 0))
    state_spec = pl.BlockSpec((None, HIST, D_ATT), lambda b, t: (b, 0, 0))
    y_p, k_p, v_p, c_p = pl.pallas_call(
        _prompt_kernel,
        grid=(batch, n_t),
        in_specs=[row_spec(D_MODEL), row_spec(D_PLE)] + weight_specs + [_const_spec(gvec.shape)],
        out_specs=[row_spec(D_MODEL), state_spec, state_spec,
                   pl.BlockSpec((None, 8, D_CONV), lambda b, t: (b, 0, 0))],
        out_shape=[jax.ShapeDtypeStruct((batch, seq, D_MODEL), F32),
                   jax.ShapeDtypeStruct((batch, HIST, D_ATT), F32),
                   jax.ShapeDtypeStruct((batch, HIST, D_ATT), F32),
                   jax.ShapeDtypeStruct((batch, 8, D_CONV), F32)],
        scratch_shapes=[pltpu.VMEM((HIST + PROMPT_TILE, D_ATT), BF16),
                        pltpu.VMEM((HIST + PROMPT_TILE, D_ATT), BF16),
                        pltpu.VMEM((8, D_CONV), F32),
                        pltpu.VMEM((8, D_CONV), F32),
                        pltpu.VMEM((N_SLABS + 1, KBLOCKS, SLAB_HEADS * GROUP, LANES), F32)],
        compiler_params=pltpu.CompilerParams(dimension_semantics=("arbitrary", "arbitrary"),
                                             vmem_limit_bytes=VMEM_LIMIT_BYTES),
        name="prompt_layer",
    )(x_prompt, p_prompt[0], *weights, gvec)

    nb = SAMPLE_BATCH_PER_STEP
    rows = nb * dec_seq
    flat_spec = lambda w: pl.BlockSpec((rows, w), lambda i: (i, 0))
    y_s, k_s, v_s, c_s = pl.pallas_call(
        _sample_kernel,
        grid=(dec_batch // nb,),
        in_specs=[flat_spec(D_MODEL), flat_spec(D_PLE),
                  pl.BlockSpec((nb, HIST, D_ATT), lambda i: (i, 0, 0)),
                  pl.BlockSpec((nb, HIST, D_ATT), lambda i: (i, 0, 0)),
                  pl.BlockSpec((nb, CONV_W - 1, D_CONV), lambda i: (i, 0, 0))]
                 + weight_specs + [_const_spec(gvec.shape)],
        out_specs=[flat_spec(D_MODEL), flat_spec(D_ATT), flat_spec(D_ATT),
                   pl.BlockSpec((nb, 8, D_CONV), lambda i: (i, 0, 0))],
        out_shape=[jax.ShapeDtypeStruct((dec_batch * dec_seq, D_MODEL), F32),
                   jax.ShapeDtypeStruct((dec_batch * dec_seq, D_ATT), F32),
                   jax.ShapeDtypeStruct((dec_batch * dec_seq, D_ATT), F32),
                   jax.ShapeDtypeStruct((dec_batch, 8, D_CONV), F32)],
        scratch_shapes=[pltpu.VMEM((nb, KWIN, D_ATT), BF16), pltpu.VMEM((nb, KWIN, D_ATT), BF16),
                        pltpu.VMEM((N_SLABS, SLAB_HEADS * dec_seq, KWIN), F32)],
        compiler_params=pltpu.CompilerParams(dimension_semantics=("arbitrary",),
                                             vmem_limit_bytes=VMEM_LIMIT_BYTES),
        name="sample_layer",
    )(x_sample.reshape(dec_batch * dec_seq, D_MODEL), p_sample[0].reshape(dec_batch * dec_seq, D_PLE),
      cache_k[0].astype(BF16).reshape(dec_batch, HIST, D_ATT),
      cache_v[0].astype(BF16).reshape(dec_batch, HIST, D_ATT),
      state_conv[0], *weights, gvec)

    return (y_p,
            y_s.reshape(dec_batch, dec_seq, D_MODEL),
            k_p.reshape(1, batch, HIST, N_HEADS, HEAD_DIM),
            v_p.reshape(1, batch, HIST, N_HEADS, HEAD_DIM),
            c_p[None, :, 8 - (CONV_W - 1):, :],
            k_s.reshape(1, dec_batch, dec_seq, N_HEADS, HEAD_DIM),
            v_s.reshape(1, dec_batch, dec_seq, N_HEADS, HEAD_DIM),
            c_s[None, :, 8 - (CONV_W - 1):, :])
```

```python
import jax
import jax.numpy as jnp
from jax import lax
from jax.experimental import pallas as pl
from jax.experimental.pallas import tpu as pltpu

D_MODEL = 1024
D_CONV = 512
D_ATT = 512
HEAD_DIM = 64
HEAD_SHIFT = 6
N_HEADS = 8
SLAB_HEADS = 4
N_SLABS = N_HEADS // SLAB_HEADS
LANES = 128
SLAB_W = SLAB_HEADS * HEAD_DIM
CHUNK = 64
N_PAST_CHUNKS = 8
HIST = N_PAST_CHUNKS * CHUNK
REL_CLIP = 128
D_PLE = 256
CONV_W = 3
EPS = 1e-6
NEG = -1e30
SCALE = HEAD_DIM ** -0.5

GROUP = 2 * CHUNK
KWIN = HIST + GROUP
KBLOCKS = KWIN // LANES
PROMPT_TILE = 512
HALF = 256
SAMPLE_BATCH_PER_STEP = 4
VMEM_LIMIT_BYTES = 56 * 1024 * 1024

F32 = jnp.float32
BF16 = jnp.bfloat16


def _rms(x, g):
    ms = jnp.mean(x * x, axis=-1, keepdims=True)
    return x * lax.rsqrt(ms + EPS) * g


def _sigmoid(x):
    return 1.0 / (1.0 + jnp.exp(-x))


def _dot(a, b):
    return jnp.dot(a, b, preferred_element_type=F32)


def _run(steps):
    try:
        while True:
            next(steps)
    except StopIteration as done:
        return done.value


def _chain(*stages):
    results = []
    for steps in stages:
        results.append((yield from steps))
    return results


def _zip(main, side, side_per_main):
    side_val, side_done = None, False
    while True:
        try:
            next(main)
        except StopIteration as done:
            main_val = done.value
            break
        for _ in range(side_per_main):
            if not side_done:
                try:
                    next(side)
                except StopIteration as done:
                    side_val, side_done = done.value, True
    if not side_done:
        side_val = _run(side)
    return main_val, side_val


def _proj(xn, win_ref, i):
    return _dot(xn, win_ref[:, 512 * i:512 * (i + 1)])


def _qkv(x, win_ref, nin_ref):
    xn = _rms(x, nin_ref[...]).astype(BF16)
    q = (_proj(xn, win_ref, 4) * SCALE).astype(BF16)
    yield
    k = _proj(xn, win_ref, 5)
    yield
    v = _proj(xn, win_ref, 6)
    return xn, q, k, v


def _conv_gate(xn, win_ref, cw_ref, ncv_ref, prev_rows):
    h = _proj(xn, win_ref, 0)
    yield
    u = _proj(xn, win_ref, 2) * h
    u1, u2 = prev_rows(u)
    cv = cw_ref[0:1, :] * u2 + cw_ref[1:2, :] * u1 + cw_ref[2:3, :] * u
    yield
    zc = _proj(xn, win_ref, 3)
    gz = zc * _sigmoid(zc)
    yield
    yc = _proj(xn, win_ref, 1) * cv * gz
    yc_n = _rms(yc, ncv_ref[...])
    yield
    za = _proj(xn, win_ref, 7)
    return u, yc_n, za


def _attend(members):
    r = members[0][0].shape[0]
    lane = lax.broadcasted_iota(jnp.int32, members[0][0].shape, 1) >> HEAD_SHIFT
    scores = []
    for qslab, kw, _, bias in members:
        zero = jnp.zeros_like(qslab)
        qs = jnp.concatenate([jnp.where(lane == j, qslab, zero) for j in range(SLAB_HEADS)], axis=0)
        s = lax.dot_general(qs, kw, (((1,), (1,)), ((), ())), preferred_element_type=F32)
        scores.append(s + bias)
    s = scores[0] if len(scores) == 1 else jnp.concatenate(scores, axis=0)
    m = jnp.max(s, axis=-1, keepdims=True)
    e = jnp.exp(s - m)
    inv_l = 1.0 / jnp.sum(e, axis=-1, keepdims=True)
    p = e.astype(BF16)
    outs = []
    for i, (_, _, vw, _) in enumerate(members):
        rows = slice(i * SLAB_HEADS * r, (i + 1) * SLAB_HEADS * r)
        o = _dot(p[rows], vw) * inv_l[rows]
        out = o[(SLAB_HEADS - 1) * r:]
        for j in reversed(range(SLAB_HEADS - 1)):
            out = jnp.where(lane <= j, o[j * r:(j + 1) * r], out)
        outs.append(out)
    return outs


def _build_bias(gvec_ref, n_rows, in_band, store):
    width = gvec_ref.shape[1]
    i = lax.broadcasted_iota(jnp.int32, (n_rows, width), 0)
    band = in_band(lax.broadcasted_iota(jnp.int32, (n_rows, KWIN), 0),
                   lax.broadcasted_iota(jnp.int32, (n_rows, KWIN), 1))
    for h in range(N_HEADS):
        x = jnp.broadcast_to(gvec_ref[h:h + 1, :], (n_rows, width))
        bit = 0
        while (1 << bit) < n_rows:
            x = jnp.where(((i >> bit) & 1) == 1, pltpu.roll(x, 1 << bit, 1), x)
            bit += 1
        store(h, jnp.where(band, x[:, GROUP:GROUP + KWIN], NEG))


def _tail(x, yc_n, ya, za, p, wout_ref, wg_ref, wp_ref, nat_ref, npl_ref, nfin_ref):
    pe = _dot(p.astype(BF16), wp_ref[...])
    ya_n = _rms(ya * (za * _sigmoid(za)), nat_ref[...])
    yield
    y = _dot(yc_n.astype(BF16), wout_ref[0:D_CONV, :]) + _dot(ya_n.astype(BF16), wout_ref[D_CONV:, :])
    x1 = x + y
    yield
    gate = _sigmoid(_dot(_rms(x1, npl_ref[...]).astype(BF16), wg_ref[...]))
    x2 = x1 + gate * pe
    return _rms(x2, nfin_ref[...])


def _prompt_kernel(x_ref, p_ref, win_ref, wout_ref, wg_ref, wp_ref, nin_ref, cw_ref, ncv_ref,
                   nat_ref, npl_ref, nfin_ref, gvec_ref,
                   y_ref, ks_ref, vs_ref, cs_ref, kbuf, vbuf, ucar, unext, bias_ref):
    t = pl.program_id(1)
    tile = x_ref.shape[0]

    @pl.when((pl.program_id(0) == 0) & (t == 0))
    def _():
        def store(h, tb):
            j = h % SLAB_HEADS
            for cb in range(KBLOCKS):
                bias_ref[h // SLAB_HEADS, cb, j * GROUP:(j + 1) * GROUP, :] = tb[:, cb * LANES:(cb + 1) * LANES]

        _build_bias(gvec_ref, GROUP,
                    lambda i, s: ((i < CHUNK) & (s < HIST + CHUNK)) | ((i >= CHUNK) & (s >= CHUNK)), store)
        bias_ref[N_SLABS] = jnp.full((KBLOCKS, SLAB_HEADS * GROUP, LANES), NEG, F32)

    @pl.when(t == 0)
    def _():
        kbuf[0:HIST, :] = jnp.zeros((HIST, D_ATT), BF16)
        vbuf[0:HIST, :] = jnp.zeros((HIST, D_ATT), BF16)
        ucar[...] = jnp.zeros(ucar.shape, F32)

    @pl.when(t > 0)
    def _():
        kbuf[0:HIST, :] = kbuf[tile:, :]
        vbuf[0:HIST, :] = vbuf[tile:, :]
        ucar[...] = unext[...]

    n_half = tile // HALF
    row = lax.broadcasted_iota(jnp.int32, (HALF, D_CONV), 0)

    def qkv(h):
        rows = slice(h * HALF, (h + 1) * HALF)
        xn, q, k, v = yield from _qkv(x_ref[rows, :], win_ref, nin_ref)
        kbuf[HIST + h * HALF:HIST + (h + 1) * HALF, :] = k.astype(BF16)
        vbuf[HIST + h * HALF:HIST + (h + 1) * HALF, :] = v.astype(BF16)
        ks_ref[rows, :] = k
        vs_ref[rows, :] = v
        return xn, q

    def conv_gate(xn, c1, c2):
        def prev_rows(u):
            u1 = jnp.where(row == 0, c1, pltpu.roll(u, 1, 0))
            u2 = jnp.where(row == 0, c2, jnp.where(row == 1, c1, pltpu.roll(u, 2, 0)))
            return u1, u2

        return _conv_gate(xn, win_ref, cw_ref, ncv_ref, prev_rows)

    def attention(h, q):
        groups = []
        for gl in range(HALF // GROUP):
            g = h * (HALF // GROUP) + gl
            first_key = t * tile + g * GROUP - HIST
            slabs = []
            for p in range(N_SLABS):
                lanes = slice(p * SLAB_W, (p + 1) * SLAB_W)
                bias = jnp.concatenate(
                    [bias_ref[jnp.where(first_key + cb * LANES < 0, N_SLABS, p), cb] for cb in range(KBLOCKS)],
                    axis=1)
                slabs.extend(_attend([(q[gl * GROUP:(gl + 1) * GROUP, lanes],
                                       kbuf[g * GROUP:g * GROUP + KWIN, lanes],
                                       vbuf[g * GROUP:g * GROUP + KWIN, lanes], bias)]))
                yield
            groups.append(jnp.concatenate(slabs, axis=1))
        return jnp.concatenate(groups, axis=0)

    def tail(h, yc_n, ya, za):
        rows = slice(h * HALF, (h + 1) * HALF)
        y_ref[rows, :] = yield from _tail(x_ref[rows, :], yc_n, ya, za, p_ref[rows, :],
                                          wout_ref, wg_ref, wp_ref, nat_ref, npl_ref, nfin_ref)

    xn, q = _run(qkv(0))
    c1, c2 = ucar[7:8, :], ucar[6:7, :]
    done = None
    for h in range(n_half):
        side = [conv_gate(xn, c1, c2)]
        if h + 1 < n_half:
            side.append(qkv(h + 1))
        if done is not None:
            side.append(tail(h - 1, *done))
        ya, results = _zip(attention(h, q), _chain(*side), 2)
        u, yc_n, za = results[0]
        done = (yc_n, ya, za)
        c1, c2 = u[HALF - 1:HALF, :], u[HALF - 2:HALF - 1, :]
        if h + 1 < n_half:
            xn, q = results[1]
    _run(tail(n_half - 1, *done))
    unext[...] = u[HALF - 8:, :]
    cs_ref[...] = u[HALF - 8:, :]


def _sample_kernel(x_ref, p_ref, ck_ref, cv_ref, sc_ref, win_ref, wout_ref, wg_ref, wp_ref, nin_ref,
                   cw_ref, ncv_ref, nat_ref, npl_ref, nfin_ref, gvec_ref,
                   y_ref, ks_ref, vs_ref, cs_ref, kbuf, vbuf, bias_ref):
    nb = ck_ref.shape[0]
    rows = x_ref.shape[0]
    seq = rows // nb

    @pl.when(pl.program_id(0) == 0)
    def _():
        def store(h, tb):
            j = h % SLAB_HEADS
            bias_ref[h // SLAB_HEADS, j * seq:(j + 1) * seq, :] = tb

        _build_bias(gvec_ref, seq, lambda i, s: s < HIST + seq, store)
        kbuf[:, HIST + seq:, :] = jnp.zeros((nb, KWIN - HIST - seq, D_ATT), BF16)
        vbuf[:, HIST + seq:, :] = jnp.zeros((nb, KWIN - HIST - seq, D_ATT), BF16)

    x = x_ref[...]
    rmod = lax.broadcasted_iota(jnp.int32, (rows, D_CONV), 0) % seq

    def prev_rows(u):
        e1 = jnp.concatenate([jnp.broadcast_to(sc_ref[j, 1:2, :], (seq, D_CONV)) for j in range(nb)], axis=0)
        e2 = jnp.concatenate([jnp.broadcast_to(sc_ref[j, 0:1, :], (seq, D_CONV)) for j in range(nb)], axis=0)
        u1 = jnp.where(rmod == 0, e1, pltpu.roll(u, 1, 0))
        u2 = jnp.where(rmod == 0, e2, jnp.where(rmod == 1, e1, pltpu.roll(u, 2, 0)))
        return u1, u2

    xn, q, k, v = _run(_qkv(x, win_ref, nin_ref))
    u, yc_n, za = _run(_conv_gate(xn, win_ref, cw_ref, ncv_ref, prev_rows))
    ks_ref[...] = k
    vs_ref[...] = v
    kb = k.astype(BF16)
    vb = v.astype(BF16)

    for j in range(nb):
        rs = slice(j * seq, (j + 1) * seq)
        cs_ref[j] = u[(j + 1) * seq - 8:(j + 1) * seq, :]
        kbuf[j, 0:HIST, :] = ck_ref[j].astype(BF16)
        vbuf[j, 0:HIST, :] = cv_ref[j].astype(BF16)
        kbuf[j, HIST:HIST + seq, :] = kb[rs, :]
        vbuf[j, HIST:HIST + seq, :] = vb[rs, :]
    slabs = []
    for p in range(N_SLABS):
        lanes = slice(p * SLAB_W, (p + 1) * SLAB_W)
        outs = _attend([(q[j * seq:(j + 1) * seq, lanes], kbuf[j, :, lanes], vbuf[j, :, lanes], bias_ref[p])
                        for j in range(nb)])
        slabs.append(jnp.concatenate(outs, axis=0))
    ya = jnp.concatenate(slabs, axis=1)
    y_ref[...] = _run(_tail(x, yc_n, ya, za, p_ref[...], wout_ref, wg_ref, wp_ref, nat_ref, npl_ref, nfin_ref))


def _bias_vector(rel_bias):
    width = KWIN + GROUP
    flat = KWIN - REL_CLIP
    assert width - flat == 2 * REL_CLIP
    return jnp.concatenate([jnp.broadcast_to(rel_bias[:, 2 * REL_CLIP:], (N_HEADS, flat)),
                            rel_bias[:, ::-1][:, :2 * REL_CLIP]], axis=1)


def _const_spec(shape):
    nd = len(shape)
    return pl.BlockSpec(shape, lambda *_: (0,) * nd, pipeline_mode=pl.Buffered(1))


def kernel(x_prompt, x_sample, cache_k, cache_v, state_conv, p_prompt, p_sample, norm_in, w_in, conv_w,
           rel_bias, norm_conv, norm_att, w_out, ple_norm, w_ple_gate, w_ple_proj, final_norm):
    depth = w_in.shape[0]
    assert depth == 1, "single-layer kernel"
    batch, seq, _ = x_prompt.shape
    dec_batch, dec_seq, _ = x_sample.shape
    kv_win = cache_k.shape[2]
    assert kv_win == HIST and PROMPT_TILE == HIST and seq % PROMPT_TILE == 0
    assert PROMPT_TILE % HALF == 0 and HALF % GROUP == 0 and (1 << HEAD_SHIFT) == HEAD_DIM
    assert dec_seq % 8 == 0 and dec_seq <= GROUP and dec_batch % SAMPLE_BATCH_PER_STEP == 0

    win = w_in[0].astype(BF16)
    wout = w_out[0].astype(BF16)
    wg = w_ple_gate[0].astype(BF16)
    wp = w_ple_proj[0].astype(BF16)
    nin = norm_in[0][None, :]
    ncv = norm_conv[0][None, :]
    nat = norm_att[0][None, :]
    npl = ple_norm[0][None, :]
    nfin = final_norm[None, :]
    cw = conv_w[0]
    gvec = _bias_vector(rel_bias[0])

    weight_specs = [
        _const_spec(win.shape), _const_spec(wout.shape), _const_spec(wg.shape), _const_spec(wp.shape),
        _const_spec(nin.shape), _const_spec(cw.shape), _const_spec(ncv.shape), _const_spec(nat.shape),
        _const_spec(npl.shape), _const_spec(nfin.shape),
    ]
    weights = (win, wout, wg, wp, nin, cw, ncv, nat, npl, nfin)

    n_t = seq // PROMPT_TILE
    row_spec = lambda w: pl.BlockSpec((None, PROMPT_TILE, w), lambda b, t: (b, t, 0))
    state_spec = pl.BlockSpec((None, HIST, D_ATT), lambda b, t: (b, 0, 0))
    y_p, k_p, v_p, c_p = pl.pallas_call(
        _prompt_kernel,
        grid=(batch, n_t),
        in_specs=[row_spec(D_MODEL), row_spec(D_PLE)] + weight_specs + [_const_spec(gvec.shape)],
        out_specs=[row_spec(D_MODEL), state_spec, state_spec,
                   pl.BlockSpec((None, 8, D_CONV), lambda b, t: (b, 0, 0))],
        out_shape=[jax.ShapeDtypeStruct((batch, seq, D_MODEL), F32),
                   jax.ShapeDtypeStruct((batch, HIST, D_ATT), F32),
                   jax.ShapeDtypeStruct((batch, HIST, D_ATT), F32),
                   jax.ShapeDtypeStruct((batch, 8, D_CONV), F32)],
        scratch_shapes=[pltpu.VMEM((HIST + PROMPT_TILE, D_ATT), BF16),
                        pltpu.VMEM((HIST + PROMPT_TILE, D_ATT), BF16),
                        pltpu.VMEM((8, D_CONV), F32),
                        pltpu.VMEM((8, D_CONV), F32),
                        pltpu.VMEM((N_SLABS + 1, KBLOCKS, SLAB_HEADS * GROUP, LANES), F32)],
        compiler_params=pltpu.CompilerParams(dimension_semantics=("arbitrary", "arbitrary"),
                                             vmem_limit_bytes=VMEM_LIMIT_BYTES),
        name="prompt_layer",
    )(x_prompt, p_prompt[0], *weights, gvec)

    nb = SAMPLE_BATCH_PER_STEP
    rows = nb * dec_seq
    flat_spec = lambda w: pl.BlockSpec((rows, w), lambda i: (i, 0))
    y_s, k_s, v_s, c_s = pl.pallas_call(
        _sample_kernel,
        grid=(dec_batch // nb,),
        in_specs=[flat_spec(D_MODEL), flat_spec(D_PLE),
                  pl.BlockSpec((nb, HIST, D_ATT), lambda i: (i, 0, 0)),
                  pl.BlockSpec((nb, HIST, D_ATT), lambda i: (i, 0, 0)),
                  pl.BlockSpec((nb, CONV_W - 1, D_CONV), lambda i: (i, 0, 0))]
                 + weight_specs + [_const_spec(gvec.shape)],
        out_specs=[flat_spec(D_MODEL), flat_spec(D_ATT), flat_spec(D_ATT),
                   pl.BlockSpec((nb, 8, D_CONV), lambda i: (i, 0, 0))],
        out_shape=[jax.ShapeDtypeStruct((dec_batch * dec_seq, D_MODEL), F32),
                   jax.ShapeDtypeStruct((dec_batch * dec_seq, D_ATT), F32),
                   jax.ShapeDtypeStruct((dec_batch * dec_seq, D_ATT), F32),
                   jax.ShapeDtypeStruct((dec_batch, 8, D_CONV), F32)],
        scratch_shapes=[pltpu.VMEM((nb, KWIN, D_ATT), BF16), pltpu.VMEM((nb, KWIN, D_ATT), BF16),
                        pltpu.VMEM((N_SLABS, SLAB_HEADS * dec_seq, KWIN), F32)],
        compiler_params=pltpu.CompilerParams(dimension_semantics=("arbitrary",),
                                             vmem_limit_bytes=VMEM_LIMIT_BYTES),
        name="sample_layer",
    )(x_sample.reshape(dec_batch * dec_seq, D_MODEL), p_sample[0].reshape(dec_batch * dec_seq, D_PLE),
      cache_k[0].reshape(dec_batch, HIST, D_ATT), cache_v[0].reshape(dec_batch, HIST, D_ATT),
      state_conv[0], *weights, gvec)

    return (y_p,
            y_s.reshape(dec_batch, dec_seq, D_MODEL),
            k_p.reshape(1, batch, HIST, N_HEADS, HEAD_DIM),
            v_p.reshape(1, batch, HIST, N_HEADS, HEAD_DIM),
            c_p[None, :, 8 - (CONV_W - 1):, :],
            k_s.reshape(1, dec_batch, dec_seq, N_HEADS, HEAD_DIM),
            v_s.reshape(1, dec_batch, dec_seq, N_HEADS, HEAD_DIM),
            c_s[None, :, 8 - (CONV_W - 1):, :])
```

```python
import jax
import jax.numpy as jnp
from jax import lax
from jax.experimental import pallas as pl
from jax.experimental.pallas import tpu as pltpu

D_MODEL = 1024
D_CONV = 512
D_ATT = 512
HEAD_DIM = 64
HEAD_SHIFT = 6
N_HEADS = 8
SLAB_HEADS = 4
N_SLABS = N_HEADS // SLAB_HEADS
LANES = 128
SLAB_W = SLAB_HEADS * HEAD_DIM
CHUNK = 64
N_PAST_CHUNKS = 8
HIST = N_PAST_CHUNKS * CHUNK
REL_CLIP = 128
D_PLE = 256
CONV_W = 3
EPS = 1e-6
NEG = -1e30
SCALE = HEAD_DIM ** -0.5

GROUP = 2 * CHUNK
KWIN = HIST + GROUP
KBLOCKS = KWIN // LANES
PROMPT_TILE = 512
HALF = 256
SAMPLE_BATCH_PER_STEP = 8
CACHE_BATCH_PER_STEP = 4
VMEM_LIMIT_BYTES = 56 * 1024 * 1024

F32 = jnp.float32
BF16 = jnp.bfloat16


def _rms(x, g):
    ms = jnp.mean(x * x, axis=-1, keepdims=True)
    return x * lax.rsqrt(ms + EPS) * g


def _sigmoid(x):
    return 1.0 / (1.0 + jnp.exp(-x))


def _dot(a, b):
    return jnp.dot(a, b, preferred_element_type=F32)


def _run(steps):
    try:
        while True:
            next(steps)
    except StopIteration as done:
        return done.value


def _chain(*stages):
    results = []
    for steps in stages:
        results.append((yield from steps))
    return results


def _zip(main, side, side_per_main):
    side_val, side_done = None, False
    while True:
        try:
            next(main)
        except StopIteration as done:
            main_val = done.value
            break
        for _ in range(side_per_main):
            if not side_done:
                try:
                    next(side)
                except StopIteration as done:
                    side_val, side_done = done.value, True
    if not side_done:
        side_val = _run(side)
    return main_val, side_val


def _proj(xn, win_ref, i):
    return _dot(xn, win_ref[:, 512 * i:512 * (i + 1)])


def _qkv(x, win_ref, nin_ref):
    xn = _rms(x, nin_ref[...]).astype(BF16)
    q = (_proj(xn, win_ref, 4) * SCALE).astype(BF16)
    yield
    k = _proj(xn, win_ref, 5)
    yield
    v = _proj(xn, win_ref, 6)
    return xn, q, k, v


def _conv_gate(xn, win_ref, cw_ref, ncv_ref, prev_rows):
    h = _proj(xn, win_ref, 0)
    yield
    u = _proj(xn, win_ref, 2) * h
    u1, u2 = prev_rows(u)
    cv = cw_ref[0:1, :] * u2 + cw_ref[1:2, :] * u1 + cw_ref[2:3, :] * u
    yield
    zc = _proj(xn, win_ref, 3)
    gz = zc * _sigmoid(zc)
    yield
    yc = _proj(xn, win_ref, 1) * cv * gz
    yc_n = _rms(yc, ncv_ref[...])
    yield
    za = _proj(xn, win_ref, 7)
    return u, yc_n, za


def _attend(members):
    r = members[0][0].shape[0]
    lane = lax.broadcasted_iota(jnp.int32, members[0][0].shape, 1) >> HEAD_SHIFT
    scores = []
    for qslab, kw, _, bias in members:
        zero = jnp.zeros_like(qslab)
        qs = jnp.concatenate([jnp.where(lane == j, qslab, zero) for j in range(SLAB_HEADS)], axis=0)
        s = lax.dot_general(qs, kw, (((1,), (1,)), ((), ())), preferred_element_type=F32)
        scores.append(s + bias)
    s = scores[0] if len(scores) == 1 else jnp.concatenate(scores, axis=0)
    m = jnp.max(s, axis=-1, keepdims=True)
    e = jnp.exp(s - m)
    inv_l = 1.0 / jnp.sum(e, axis=-1, keepdims=True)
    p = e.astype(BF16)
    outs = []
    for i, (_, _, vw, _) in enumerate(members):
        rows = slice(i * SLAB_HEADS * r, (i + 1) * SLAB_HEADS * r)
        o = _dot(p[rows], vw) * inv_l[rows]
        out = o[(SLAB_HEADS - 1) * r:]
        for j in reversed(range(SLAB_HEADS - 1)):
            out = jnp.where(lane <= j, o[j * r:(j + 1) * r], out)
        outs.append(out)
    return outs


def _build_bias(gvec_ref, n_rows, in_band, store):
    width = gvec_ref.shape[1]
    i = lax.broadcasted_iota(jnp.int32, (n_rows, width), 0)
    band = in_band(lax.broadcasted_iota(jnp.int32, (n_rows, KWIN), 0),
                   lax.broadcasted_iota(jnp.int32, (n_rows, KWIN), 1))
    for h in range(N_HEADS):
        x = jnp.broadcast_to(gvec_ref[h:h + 1, :], (n_rows, width))
        bit = 0
        while (1 << bit) < n_rows:
            x = jnp.where(((i >> bit) & 1) == 1, pltpu.roll(x, 1 << bit, 1), x)
            bit += 1
        store(h, jnp.where(band, x[:, GROUP:GROUP + KWIN], NEG))


def _tail(x, yc_n, ya, za, p, wout_ref, wg_ref, wp_ref, nat_ref, npl_ref, nfin_ref):
    pe = _dot(p.astype(BF16), wp_ref[...])
    ya_n = _rms(ya * (za * _sigmoid(za)), nat_ref[...])
    yield
    y = _dot(yc_n.astype(BF16), wout_ref[0:D_CONV, :]) + _dot(ya_n.astype(BF16), wout_ref[D_CONV:, :])
    x1 = x + y
    yield
    gate = _sigmoid(_dot(_rms(x1, npl_ref[...]).astype(BF16), wg_ref[...]))
    x2 = x1 + gate * pe
    return _rms(x2, nfin_ref[...])


def _prompt_kernel(x_ref, p_ref, win_ref, wout_ref, wg_ref, wp_ref, nin_ref, cw_ref, ncv_ref,
                   nat_ref, npl_ref, nfin_ref, gvec_ref,
                   y_ref, ks_ref, vs_ref, cs_ref, kbuf, vbuf, ucar, unext, bias_ref):
    t = pl.program_id(1)
    tile = x_ref.shape[0]

    @pl.when((pl.program_id(0) == 0) & (t == 0))
    def _():
        def store(h, tb):
            j = h % SLAB_HEADS
            for cb in range(KBLOCKS):
                bias_ref[h // SLAB_HEADS, cb, j * GROUP:(j + 1) * GROUP, :] = tb[:, cb * LANES:(cb + 1) * LANES]

        _build_bias(gvec_ref, GROUP,
                    lambda i, s: ((i < CHUNK) & (s < HIST + CHUNK)) | ((i >= CHUNK) & (s >= CHUNK)), store)
        bias_ref[N_SLABS] = jnp.full((KBLOCKS, SLAB_HEADS * GROUP, LANES), NEG, F32)

    @pl.when(t == 0)
    def _():
        kbuf[0:HIST, :] = jnp.zeros((HIST, D_ATT), BF16)
        vbuf[0:HIST, :] = jnp.zeros((HIST, D_ATT), BF16)
        ucar[...] = jnp.zeros(ucar.shape, F32)

    @pl.when(t > 0)
    def _():
        kbuf[0:HIST, :] = kbuf[tile:, :]
        vbuf[0:HIST, :] = vbuf[tile:, :]
        ucar[...] = unext[...]

    n_half = tile // HALF
    row = lax.broadcasted_iota(jnp.int32, (HALF, D_CONV), 0)

    def qkv(h):
        rows = slice(h * HALF, (h + 1) * HALF)
        xn, q, k, v = yield from _qkv(x_ref[rows, :], win_ref, nin_ref)
        kbuf[HIST + h * HALF:HIST + (h + 1) * HALF, :] = k.astype(BF16)
        vbuf[HIST + h * HALF:HIST + (h + 1) * HALF, :] = v.astype(BF16)
        ks_ref[rows, :] = k
        vs_ref[rows, :] = v
        return xn, q

    def conv_gate(xn, c1, c2):
        def prev_rows(u):
            u1 = jnp.where(row == 0, c1, pltpu.roll(u, 1, 0))
            u2 = jnp.where(row == 0, c2, jnp.where(row == 1, c1, pltpu.roll(u, 2, 0)))
            return u1, u2

        return _conv_gate(xn, win_ref, cw_ref, ncv_ref, prev_rows)

    def attention(h, q):
        groups = []
        for gl in range(HALF // GROUP):
            g = h * (HALF // GROUP) + gl
            first_key = t * tile + g * GROUP - HIST
            slabs = []
            for p in range(N_SLABS):
                lanes = slice(p * SLAB_W, (p + 1) * SLAB_W)
                bias = jnp.concatenate(
                    [bias_ref[jnp.where(first_key + cb * LANES < 0, N_SLABS, p), cb] for cb in range(KBLOCKS)],
                    axis=1)
                slabs.extend(_attend([(q[gl * GROUP:(gl + 1) * GROUP, lanes],
                                       kbuf[g * GROUP:g * GROUP + KWIN, lanes],
                                       vbuf[g * GROUP:g * GROUP + KWIN, lanes], bias)]))
                yield
            groups.append(jnp.concatenate(slabs, axis=1))
        return jnp.concatenate(groups, axis=0)

    def tail(h, yc_n, ya, za):
        rows = slice(h * HALF, (h + 1) * HALF)
        y_ref[rows, :] = yield from _tail(x_ref[rows, :], yc_n, ya, za, p_ref[rows, :],
                                          wout_ref, wg_ref, wp_ref, nat_ref, npl_ref, nfin_ref)

    xn, q = _run(qkv(0))
    c1, c2 = ucar[7:8, :], ucar[6:7, :]
    done = None
    for h in range(n_half):
        side = [conv_gate(xn, c1, c2)]
        if h + 1 < n_half:
            side.append(qkv(h + 1))
        if done is not None:
            side.append(tail(h - 1, *done))
        ya, results = _zip(attention(h, q), _chain(*side), 2)
        u, yc_n, za = results[0]
        done = (yc_n, ya, za)
        c1, c2 = u[HALF - 1:HALF, :], u[HALF - 2:HALF - 1, :]
        if h + 1 < n_half:
            xn, q = results[1]
    _run(tail(n_half - 1, *done))
    unext[...] = u[HALF - 8:, :]
    cs_ref[...] = u[HALF - 8:, :]


def _sample_kernel(x_ref, p_ref, ck_ref, cv_ref, sc_ref, win_ref, wout_ref, wg_ref, wp_ref, nin_ref,
                   cw_ref, ncv_ref, nat_ref, npl_ref, nfin_ref, gvec_ref,
                   y_ref, ks_ref, vs_ref, cs_ref, kbuf, vbuf, bias_ref):
    nb = ck_ref.shape[0]
    rows = x_ref.shape[0]
    seq = rows // nb

    @pl.when(pl.program_id(0) == 0)
    def _():
        def store(h, tb):
            j = h % SLAB_HEADS
            bias_ref[h // SLAB_HEADS, j * seq:(j + 1) * seq, :] = tb

        _build_bias(gvec_ref, seq, lambda i, s: s < HIST + seq, store)
        kbuf[:, HIST + seq:, :] = jnp.zeros((nb, KWIN - HIST - seq, D_ATT), BF16)
        vbuf[:, HIST + seq:, :] = jnp.zeros((nb, KWIN - HIST - seq, D_ATT), BF16)

    x = x_ref[...]
    rmod = lax.broadcasted_iota(jnp.int32, (rows, D_CONV), 0) % seq

    def prev_rows(u):
        e1 = jnp.concatenate([jnp.broadcast_to(sc_ref[j, 1:2, :], (seq, D_CONV)) for j in range(nb)], axis=0)
        e2 = jnp.concatenate([jnp.broadcast_to(sc_ref[j, 0:1, :], (seq, D_CONV)) for j in range(nb)], axis=0)
        u1 = jnp.where(rmod == 0, e1, pltpu.roll(u, 1, 0))
        u2 = jnp.where(rmod == 0, e2, jnp.where(rmod == 1, e1, pltpu.roll(u, 2, 0)))
        return u1, u2

    xn, q, k, v = _run(_qkv(x, win_ref, nin_ref))
    u, yc_n, za = _run(_conv_gate(xn, win_ref, cw_ref, ncv_ref, prev_rows))
    ks_ref[...] = k
    vs_ref[...] = v
    kb = k.astype(BF16)
    vb = v.astype(BF16)

    for j in range(nb):
        rs = slice(j * seq, (j + 1) * seq)
        cs_ref[j] = u[(j + 1) * seq - 8:(j + 1) * seq, :]
        kbuf[j, 0:HIST, :] = ck_ref[j]
        vbuf[j, 0:HIST, :] = cv_ref[j]
        kbuf[j, HIST:HIST + seq, :] = kb[rs, :]
        vbuf[j, HIST:HIST + seq, :] = vb[rs, :]
    slabs = []
    for p in range(N_SLABS):
        lanes = slice(p * SLAB_W, (p + 1) * SLAB_W)
        outs = _attend([(q[j * seq:(j + 1) * seq, lanes], kbuf[j, :, lanes], vbuf[j, :, lanes], bias_ref[p])
                        for j in range(nb)])
        slabs.append(jnp.concatenate(outs, axis=0))
    ya = jnp.concatenate(slabs, axis=1)
    y_ref[...] = _run(_tail(x, yc_n, ya, za, p_ref[...], wout_ref, wg_ref, wp_ref, nat_ref, npl_ref, nfin_ref))


def _cache_rows_kernel(ck_ref, cv_ref, ok_ref, ov_ref):
    nb = ck_ref.shape[0]
    for j in range(nb):
        ok_ref[j] = ck_ref[j].reshape(HIST, D_ATT).astype(BF16)
        ov_ref[j] = cv_ref[j].reshape(HIST, D_ATT).astype(BF16)


def _cache_rows(cache_k, cache_v):
    dec_batch = cache_k.shape[0]
    nb = CACHE_BATCH_PER_STEP
    in_spec = pl.BlockSpec((nb, HIST, N_HEADS, HEAD_DIM), lambda i: (i, 0, 0, 0))
    out_spec = pl.BlockSpec((nb, HIST, D_ATT), lambda i: (i, 0, 0))
    out = jax.ShapeDtypeStruct((dec_batch, HIST, D_ATT), BF16)
    return pl.pallas_call(
        _cache_rows_kernel,
        grid=(dec_batch // nb,),
        in_specs=[in_spec, in_spec],
        out_specs=[out_spec, out_spec],
        out_shape=[out, out],
        compiler_params=pltpu.CompilerParams(dimension_semantics=("arbitrary",),
                                             vmem_limit_bytes=VMEM_LIMIT_BYTES),
        name="cache_rows",
    )(cache_k, cache_v)


def _bias_vector(rel_bias):
    width = KWIN + GROUP
    flat = KWIN - REL_CLIP
    assert width - flat == 2 * REL_CLIP
    return jnp.concatenate([jnp.broadcast_to(rel_bias[:, 2 * REL_CLIP:], (N_HEADS, flat)),
                            rel_bias[:, ::-1][:, :2 * REL_CLIP]], axis=1)


def _const_spec(shape):
    nd = len(shape)
    return pl.BlockSpec(shape, lambda *_: (0,) * nd, pipeline_mode=pl.Buffered(1))


def kernel(x_prompt, x_sample, cache_k, cache_v, state_conv, p_prompt, p_sample, norm_in, w_in, conv_w,
           rel_bias, norm_conv, norm_att, w_out, ple_norm, w_ple_gate, w_ple_proj, final_norm):
    depth = w_in.shape[0]
    assert depth == 1, "single-layer kernel"
    batch, seq, _ = x_prompt.shape
    dec_batch, dec_seq, _ = x_sample.shape
    kv_win = cache_k.shape[2]
    assert kv_win == HIST and PROMPT_TILE == HIST and seq % PROMPT_TILE == 0
    assert PROMPT_TILE % HALF == 0 and HALF % GROUP == 0 and (1 << HEAD_SHIFT) == HEAD_DIM
    assert dec_seq % 8 == 0 and dec_seq <= GROUP
    assert dec_batch % SAMPLE_BATCH_PER_STEP == 0 and dec_batch % CACHE_BATCH_PER_STEP == 0

    win = w_in[0].astype(BF16)
    wout = w_out[0].astype(BF16)
    wg = w_ple_gate[0].astype(BF16)
    wp = w_ple_proj[0].astype(BF16)
    nin = norm_in[0][None, :]
    ncv = norm_conv[0][None, :]
    nat = norm_att[0][None, :]
    npl = ple_norm[0][None, :]
    nfin = final_norm[None, :]
    cw = conv_w[0]
    gvec = _bias_vector(rel_bias[0])

    weight_specs = [
        _const_spec(win.shape), _const_spec(wout.shape), _const_spec(wg.shape), _const_spec(wp.shape),
        _const_spec(nin.shape), _const_spec(cw.shape), _const_spec(ncv.shape), _const_spec(nat.shape),
        _const_spec(npl.shape), _const_spec(nfin.shape),
    ]
    weights = (win, wout, wg, wp, nin, cw, ncv, nat, npl, nfin)

    n_t = seq // PROMPT_TILE
    row_spec = lambda w: pl.BlockSpec((None, PROMPT_TILE, w), lambda b, t: (b, t, 0))
    state_spec = pl.BlockSpec((None, HIST, D_ATT), lambda b, t: (b, 0, 0))
    y_p, k_p, v_p, c_p = pl.pallas_call(
        _prompt_kernel,
        grid=(batch, n_t),
        in_specs=[row_spec(D_MODEL), row_spec(D_PLE)] + weight_specs + [_const_spec(gvec.shape)],
        out_specs=[row_spec(D_MODEL), state_spec, state_spec,
                   pl.BlockSpec((None, 8, D_CONV), lambda b, t: (b, 0, 0))],
        out_shape=[jax.ShapeDtypeStruct((batch, seq, D_MODEL), F32),
                   jax.ShapeDtypeStruct((batch, HIST, D_ATT), F32),
                   jax.ShapeDtypeStruct((batch, HIST, D_ATT), F32),
                   jax.ShapeDtypeStruct((batch, 8, D_CONV), F32)],
        scratch_shapes=[pltpu.VMEM((HIST + PROMPT_TILE, D_ATT), BF16),
                        pltpu.VMEM((HIST + PROMPT_TILE, D_ATT), BF16),
                        pltpu.VMEM((8, D_CONV), F32),
                        pltpu.VMEM((8, D_CONV), F32),
                        pltpu.VMEM((N_SLABS + 1, KBLOCKS, SLAB_HEADS * GROUP, LANES), F32)],
        compiler_params=pltpu.CompilerParams(dimension_semantics=("arbitrary", "arbitrary"),
                                             vmem_limit_bytes=VMEM_LIMIT_BYTES),
        name="prompt_layer",
    )(x_prompt, p_prompt[0], *weights, gvec)

    ck_rows, cv_rows = _cache_rows(cache_k[0], cache_v[0])
    nb = SAMPLE_BATCH_PER_STEP
    rows = nb * dec_seq
    flat_spec = lambda w: pl.BlockSpec((rows, w), lambda i: (i, 0))
    y_s, k_s, v_s, c_s = pl.pallas_call(
        _sample_kernel,
        grid=(dec_batch // nb,),
        in_specs=[flat_spec(D_MODEL), flat_spec(D_PLE),
                  pl.BlockSpec((nb, HIST, D_ATT), lambda i: (i, 0, 0)),
                  pl.BlockSpec((nb, HIST, D_ATT), lambda i: (i, 0, 0)),
                  pl.BlockSpec((nb, CONV_W - 1, D_CONV), lambda i: (i, 0, 0))]
                 + weight_specs + [_const_spec(gvec.shape)],
        out_specs=[flat_spec(D_MODEL), flat_spec(D_ATT), flat_spec(D_ATT),
                   pl.BlockSpec((nb, 8, D_CONV), lambda i: (i, 0, 0))],
        out_shape=[jax.ShapeDtypeStruct((dec_batch * dec_seq, D_MODEL), F32),
                   jax.ShapeDtypeStruct((dec_batch * dec_seq, D_ATT), F32),
                   jax.ShapeDtypeStruct((dec_batch * dec_seq, D_ATT), F32),
                   jax.ShapeDtypeStruct((dec_batch, 8, D_CONV), F32)],
        scratch_shapes=[pltpu.VMEM((nb, KWIN, D_ATT), BF16), pltpu.VMEM((nb, KWIN, D_ATT), BF16),
                        pltpu.VMEM((N_SLABS, SLAB_HEADS * dec_seq, KWIN), F32)],
        compiler_params=pltpu.CompilerParams(dimension_semantics=("arbitrary",),
                                             vmem_limit_bytes=VMEM_LIMIT_BYTES),
        name="sample_layer",
    )(x_sample.reshape(dec_batch * dec_seq, D_MODEL), p_sample[0].reshape(dec_batch * dec_seq, D_PLE),
      ck_rows, cv_rows, state_conv[0], *weights, gvec)

    return (y_p,
            y_s.reshape(dec_batch, dec_seq, D_MODEL),
            k_p.reshape(1, batch, HIST, N_HEADS, HEAD_DIM),
            v_p.reshape(1, batch, HIST, N_HEADS, HEAD_DIM),
            c_p[None, :, 8 - (CONV_W - 1):, :],
            k_s.reshape(1, dec_batch, dec_seq, N_HEADS, HEAD_DIM),
            v_s.reshape(1, dec_batch, dec_seq, N_HEADS, HEAD_DIM),
            c_s[None, :, 8 - (CONV_W - 1):, :])
```

```python
import jax
import jax.numpy as jnp
from jax import lax
from jax.experimental import pallas as pl
from jax.experimental.pallas import tpu as pltpu

D_MODEL = 1024
D_CONV = 512
D_ATT = 512
HEAD_DIM = 64
HEAD_SHIFT = 6
N_HEADS = 8
SLAB_HEADS = 4
N_SLABS = N_HEADS // SLAB_HEADS
LANES = 128
SLAB_W = SLAB_HEADS * HEAD_DIM
CHUNK = 64
N_PAST_CHUNKS = 8
HIST = N_PAST_CHUNKS * CHUNK
REL_CLIP = 128
D_PLE = 256
CONV_W = 3
EPS = 1e-6
NEG = -1e30
SCALE = HEAD_DIM ** -0.5

GROUP = 2 * CHUNK
KWIN = HIST + GROUP
KBLOCKS = KWIN // LANES
PROMPT_TILE = 512
HALF = 256
SAMPLE_BATCH_PER_STEP = 4
VMEM_LIMIT_BYTES = 56 * 1024 * 1024

F32 = jnp.float32
BF16 = jnp.bfloat16


def _rms(x, g):
    ms = jnp.mean(x * x, axis=-1, keepdims=True)
    return x * lax.rsqrt(ms + EPS) * g


def _sigmoid(x):
    return 1.0 / (1.0 + jnp.exp(-x))


def _dot(a, b):
    return jnp.dot(a, b, preferred_element_type=F32)


def _run(steps):
    try:
        while True:
            next(steps)
    except StopIteration as done:
        return done.value


def _chain(*stages):
    results = []
    for steps in stages:
        results.append((yield from steps))
    return results


def _zip(main, side, side_per_main):
    side_val, side_done = None, False
    while True:
        try:
            next(main)
        except StopIteration as done:
            main_val = done.value
            break
        for _ in range(side_per_main):
            if not side_done:
                try:
                    next(side)
                except StopIteration as done:
                    side_val, side_done = done.value, True
    if not side_done:
        side_val = _run(side)
    return main_val, side_val


def _proj(xn, win_ref, i):
    return _dot(xn, win_ref[:, 512 * i:512 * (i + 1)])


def _qkv(x, win_ref, nin_ref):
    xn = _rms(x, nin_ref[...]).astype(BF16)
    q = (_proj(xn, win_ref, 4) * SCALE).astype(BF16)
    yield
    k = _proj(xn, win_ref, 5)
    yield
    v = _proj(xn, win_ref, 6)
    return xn, q, k, v


def _conv_gate(xn, win_ref, cw_ref, ncv_ref, prev_rows):
    h = _proj(xn, win_ref, 0)
    yield
    u = _proj(xn, win_ref, 2) * h
    u1, u2 = prev_rows(u)
    cv = cw_ref[0:1, :] * u2 + cw_ref[1:2, :] * u1 + cw_ref[2:3, :] * u
    yield
    zc = _proj(xn, win_ref, 3)
    gz = zc * _sigmoid(zc)
    yield
    yc = _proj(xn, win_ref, 1) * cv * gz
    yc_n = _rms(yc, ncv_ref[...])
    yield
    za = _proj(xn, win_ref, 7)
    return u, yc_n, za


def _dot_t(a, b):
    return lax.dot_general(a, b, (((1,), (1,)), ((), ())), preferred_element_type=F32)


def _attend(members):
    r = members[0][0].shape[0]
    lane = lax.broadcasted_iota(jnp.int32, members[0][0].shape, 1) >> HEAD_SHIFT
    scores = []
    for qslab, keys, _, bias in members:
        zero = jnp.zeros_like(qslab)
        qs = jnp.concatenate([jnp.where(lane == j, qslab, zero) for j in range(SLAB_HEADS)], axis=0)
        parts = [_dot(qs, k) if transposed else _dot_t(qs, k) for k, transposed in keys]
        s = parts[0] if len(parts) == 1 else jnp.concatenate(parts, axis=1)
        scores.append(s + bias)
    s = scores[0] if len(scores) == 1 else jnp.concatenate(scores, axis=0)
    m = jnp.max(s, axis=-1, keepdims=True)
    e = jnp.exp(s - m)
    inv_l = 1.0 / jnp.sum(e, axis=-1, keepdims=True)
    p = e.astype(BF16)
    outs = []
    for i, (_, _, values, _) in enumerate(members):
        rows = slice(i * SLAB_HEADS * r, (i + 1) * SLAB_HEADS * r)
        o, col = None, 0
        for v, transposed in values:
            n = v.shape[1] if transposed else v.shape[0]
            pv = _dot_t(p[rows, col:col + n], v) if transposed else _dot(p[rows, col:col + n], v)
            o = pv if o is None else o + pv
            col += n
        o = o * inv_l[rows]
        out = o[(SLAB_HEADS - 1) * r:]
        for j in reversed(range(SLAB_HEADS - 1)):
            out = jnp.where(lane <= j, o[j * r:(j + 1) * r], out)
        outs.append(out)
    return outs


def _build_bias(gvec_ref, n_rows, in_band, store):
    width = gvec_ref.shape[1]
    i = lax.broadcasted_iota(jnp.int32, (n_rows, width), 0)
    band = in_band(lax.broadcasted_iota(jnp.int32, (n_rows, KWIN), 0),
                   lax.broadcasted_iota(jnp.int32, (n_rows, KWIN), 1))
    for h in range(N_HEADS):
        x = jnp.broadcast_to(gvec_ref[h:h + 1, :], (n_rows, width))
        bit = 0
        while (1 << bit) < n_rows:
            x = jnp.where(((i >> bit) & 1) == 1, pltpu.roll(x, 1 << bit, 1), x)
            bit += 1
        store(h, jnp.where(band, x[:, GROUP:GROUP + KWIN], NEG))


def _tail(x, yc_n, ya, za, p, wout_ref, wg_ref, wp_ref, nat_ref, npl_ref, nfin_ref):
    pe = _dot(p.astype(BF16), wp_ref[...])
    ya_n = _rms(ya * (za * _sigmoid(za)), nat_ref[...])
    yield
    y = _dot(yc_n.astype(BF16), wout_ref[0:D_CONV, :]) + _dot(ya_n.astype(BF16), wout_ref[D_CONV:, :])
    x1 = x + y
    yield
    gate = _sigmoid(_dot(_rms(x1, npl_ref[...]).astype(BF16), wg_ref[...]))
    x2 = x1 + gate * pe
    return _rms(x2, nfin_ref[...])


def _prompt_kernel(x_ref, p_ref, win_ref, wout_ref, wg_ref, wp_ref, nin_ref, cw_ref, ncv_ref,
                   nat_ref, npl_ref, nfin_ref, gvec_ref,
                   y_ref, ks_ref, vs_ref, cs_ref, kbuf, vbuf, ucar, unext, bias_ref):
    t = pl.program_id(1)
    tile = x_ref.shape[0]

    @pl.when((pl.program_id(0) == 0) & (t == 0))
    def _():
        def store(h, tb):
            j = h % SLAB_HEADS
            for cb in range(KBLOCKS):
                bias_ref[h // SLAB_HEADS, cb, j * GROUP:(j + 1) * GROUP, :] = tb[:, cb * LANES:(cb + 1) * LANES]

        _build_bias(gvec_ref, GROUP,
                    lambda i, s: ((i < CHUNK) & (s < HIST + CHUNK)) | ((i >= CHUNK) & (s >= CHUNK)), store)
        bias_ref[N_SLABS] = jnp.full((KBLOCKS, SLAB_HEADS * GROUP, LANES), NEG, F32)

    @pl.when(t == 0)
    def _():
        kbuf[0:HIST, :] = jnp.zeros((HIST, D_ATT), BF16)
        vbuf[0:HIST, :] = jnp.zeros((HIST, D_ATT), BF16)
        ucar[...] = jnp.zeros(ucar.shape, F32)

    @pl.when(t > 0)
    def _():
        kbuf[0:HIST, :] = kbuf[tile:, :]
        vbuf[0:HIST, :] = vbuf[tile:, :]
        ucar[...] = unext[...]

    n_half = tile // HALF
    row = lax.broadcasted_iota(jnp.int32, (HALF, D_CONV), 0)

    def qkv(h):
        rows = slice(h * HALF, (h + 1) * HALF)
        xn, q, k, v = yield from _qkv(x_ref[rows, :], win_ref, nin_ref)
        kbuf[HIST + h * HALF:HIST + (h + 1) * HALF, :] = k.astype(BF16)
        vbuf[HIST + h * HALF:HIST + (h + 1) * HALF, :] = v.astype(BF16)
        ks_ref[rows, :] = k
        vs_ref[rows, :] = v
        return xn, q

    def conv_gate(xn, c1, c2):
        def prev_rows(u):
            u1 = jnp.where(row == 0, c1, pltpu.roll(u, 1, 0))
            u2 = jnp.where(row == 0, c2, jnp.where(row == 1, c1, pltpu.roll(u, 2, 0)))
            return u1, u2

        return _conv_gate(xn, win_ref, cw_ref, ncv_ref, prev_rows)

    def attention(h, q):
        groups = []
        for gl in range(HALF // GROUP):
            g = h * (HALF // GROUP) + gl
            first_key = t * tile + g * GROUP - HIST
            slabs = []
            for p in range(N_SLABS):
                lanes = slice(p * SLAB_W, (p + 1) * SLAB_W)
                bias = jnp.concatenate(
                    [bias_ref[jnp.where(first_key + cb * LANES < 0, N_SLABS, p), cb] for cb in range(KBLOCKS)],
                    axis=1)
                slabs.extend(_attend([(q[gl * GROUP:(gl + 1) * GROUP, lanes],
                                       [(kbuf[g * GROUP:g * GROUP + KWIN, lanes], False)],
                                       [(vbuf[g * GROUP:g * GROUP + KWIN, lanes], False)], bias)]))
                yield
            groups.append(jnp.concatenate(slabs, axis=1))
        return jnp.concatenate(groups, axis=0)

    def tail(h, yc_n, ya, za):
        rows = slice(h * HALF, (h + 1) * HALF)
        y_ref[rows, :] = yield from _tail(x_ref[rows, :], yc_n, ya, za, p_ref[rows, :],
                                          wout_ref, wg_ref, wp_ref, nat_ref, npl_ref, nfin_ref)

    xn, q = _run(qkv(0))
    c1, c2 = ucar[7:8, :], ucar[6:7, :]
    done = None
    for h in range(n_half):
        side = [conv_gate(xn, c1, c2)]
        if h + 1 < n_half:
            side.append(qkv(h + 1))
        if done is not None:
            side.append(tail(h - 1, *done))
        ya, results = _zip(attention(h, q), _chain(*side), 2)
        u, yc_n, za = results[0]
        done = (yc_n, ya, za)
        c1, c2 = u[HALF - 1:HALF, :], u[HALF - 2:HALF - 1, :]
        if h + 1 < n_half:
            xn, q = results[1]
    _run(tail(n_half - 1, *done))
    unext[...] = u[HALF - 8:, :]
    cs_ref[...] = u[HALF - 8:, :]


def _sample_kernel(x_ref, p_ref, ck_ref, cv_ref, sc_ref, win_ref, wout_ref, wg_ref, wp_ref, nin_ref,
                   cw_ref, ncv_ref, nat_ref, npl_ref, nfin_ref, gvec_ref,
                   y_ref, ks_ref, vs_ref, cs_ref, knew, vnew, bias_ref):
    nb = ck_ref.shape[0]
    rows = x_ref.shape[0]
    seq = rows // nb

    @pl.when(pl.program_id(0) == 0)
    def _():
        def store(h, tb):
            j = h % SLAB_HEADS
            bias_ref[h // SLAB_HEADS, j * seq:(j + 1) * seq, :] = tb

        _build_bias(gvec_ref, seq, lambda i, s: s < HIST + seq, store)
        knew[:, seq:, :] = jnp.zeros((nb, GROUP - seq, D_ATT), BF16)
        vnew[:, seq:, :] = jnp.zeros((nb, GROUP - seq, D_ATT), BF16)

    x = x_ref[...]
    rmod = lax.broadcasted_iota(jnp.int32, (rows, D_CONV), 0) % seq

    def prev_rows(u):
        e1 = jnp.concatenate([jnp.broadcast_to(sc_ref[j, 1:2, :], (seq, D_CONV)) for j in range(nb)], axis=0)
        e2 = jnp.concatenate([jnp.broadcast_to(sc_ref[j, 0:1, :], (seq, D_CONV)) for j in range(nb)], axis=0)
        u1 = jnp.where(rmod == 0, e1, pltpu.roll(u, 1, 0))
        u2 = jnp.where(rmod == 0, e2, jnp.where(rmod == 1, e1, pltpu.roll(u, 2, 0)))
        return u1, u2

    xn, q, k, v = _run(_qkv(x, win_ref, nin_ref))
    u, yc_n, za = _run(_conv_gate(xn, win_ref, cw_ref, ncv_ref, prev_rows))
    ks_ref[...] = k
    vs_ref[...] = v
    kb = k.astype(BF16)
    vb = v.astype(BF16)

    cache_kt, cache_vt = [], []
    for j in range(nb):
        rs = slice(j * seq, (j + 1) * seq)
        cs_ref[j] = u[(j + 1) * seq - 8:(j + 1) * seq, :]
        knew[j, 0:seq, :] = kb[rs, :]
        vnew[j, 0:seq, :] = vb[rs, :]
        cache_kt.append(ck_ref[j].reshape(D_ATT, HIST).astype(BF16))
        cache_vt.append(cv_ref[j].reshape(D_ATT, HIST).astype(BF16))
    slabs = []
    for p in range(N_SLABS):
        lanes = slice(p * SLAB_W, (p + 1) * SLAB_W)
        outs = _attend([(q[j * seq:(j + 1) * seq, lanes],
                         [(cache_kt[j][lanes, :], True), (knew[j, :, lanes], False)],
                         [(cache_vt[j][lanes, :], True), (vnew[j, :, lanes], False)], bias_ref[p])
                        for j in range(nb)])
        slabs.append(jnp.concatenate(outs, axis=0))
    ya = jnp.concatenate(slabs, axis=1)
    y_ref[...] = _run(_tail(x, yc_n, ya, za, p_ref[...], wout_ref, wg_ref, wp_ref, nat_ref, npl_ref, nfin_ref))


def _bias_vector(rel_bias):
    width = KWIN + GROUP
    flat = KWIN - REL_CLIP
    assert width - flat == 2 * REL_CLIP
    return jnp.concatenate([jnp.broadcast_to(rel_bias[:, 2 * REL_CLIP:], (N_HEADS, flat)),
                            rel_bias[:, ::-1][:, :2 * REL_CLIP]], axis=1)


def _const_spec(shape):
    nd = len(shape)
    return pl.BlockSpec(shape, lambda *_: (0,) * nd, pipeline_mode=pl.Buffered(1))


def kernel(x_prompt, x_sample, cache_k, cache_v, state_conv, p_prompt, p_sample, norm_in, w_in, conv_w,
           rel_bias, norm_conv, norm_att, w_out, ple_norm, w_ple_gate, w_ple_proj, final_norm):
    depth = w_in.shape[0]
    assert depth == 1, "single-layer kernel"
    batch, seq, _ = x_prompt.shape
    dec_batch, dec_seq, _ = x_sample.shape
    kv_win = cache_k.shape[2]
    assert kv_win == HIST and PROMPT_TILE == HIST and seq % PROMPT_TILE == 0
    assert PROMPT_TILE % HALF == 0 and HALF % GROUP == 0 and (1 << HEAD_SHIFT) == HEAD_DIM
    assert dec_seq % 8 == 0 and dec_seq <= GROUP and dec_batch % SAMPLE_BATCH_PER_STEP == 0

    win = w_in[0].astype(BF16)
    wout = w_out[0].astype(BF16)
    wg = w_ple_gate[0].astype(BF16)
    wp = w_ple_proj[0].astype(BF16)
    nin = norm_in[0][None, :]
    ncv = norm_conv[0][None, :]
    nat = norm_att[0][None, :]
    npl = ple_norm[0][None, :]
    nfin = final_norm[None, :]
    cw = conv_w[0]
    gvec = _bias_vector(rel_bias[0])

    weight_specs = [
        _const_spec(win.shape), _const_spec(wout.shape), _const_spec(wg.shape), _const_spec(wp.shape),
        _const_spec(nin.shape), _const_spec(cw.shape), _const_spec(ncv.shape), _const_spec(nat.shape),
        _const_spec(npl.shape), _const_spec(nfin.shape),
    ]
    weights = (win, wout, wg, wp, nin, cw, ncv, nat, npl, nfin)

    n_t = seq // PROMPT_TILE
    row_spec = lambda w: pl.BlockSpec((None, PROMPT_TILE, w), lambda b, t: (b, t, 0))
    state_spec = pl.BlockSpec((None, HIST, D_ATT), lambda b, t: (b, 0, 0))
    y_p, k_p, v_p, c_p = pl.pallas_call(
        _prompt_kernel,
        grid=(batch, n_t),
        in_specs=[row_spec(D_MODEL), row_spec(D_PLE)] + weight_specs + [_const_spec(gvec.shape)],
        out_specs=[row_spec(D_MODEL), state_spec, state_spec,
                   pl.BlockSpec((None, 8, D_CONV), lambda b, t: (b, 0, 0))],
        out_shape=[jax.ShapeDtypeStruct((batch, seq, D_MODEL), F32),
                   jax.ShapeDtypeStruct((batch, HIST, D_ATT), F32),
                   jax.ShapeDtypeStruct((batch, HIST, D_ATT), F32),
                   jax.ShapeDtypeStruct((batch, 8, D_CONV), F32)],
        scratch_shapes=[pltpu.VMEM((HIST + PROMPT_TILE, D_ATT), BF16),
                        pltpu.VMEM((HIST + PROMPT_TILE, D_ATT), BF16),
                        pltpu.VMEM((8, D_CONV), F32),
                        pltpu.VMEM((8, D_CONV), F32),
                        pltpu.VMEM((N_SLABS + 1, KBLOCKS, SLAB_HEADS * GROUP, LANES), F32)],
        compiler_params=pltpu.CompilerParams(dimension_semantics=("arbitrary", "arbitrary"),
                                             vmem_limit_bytes=VMEM_LIMIT_BYTES),
        name="prompt_layer",
    )(x_prompt, p_prompt[0], *weights, gvec)

    nb = SAMPLE_BATCH_PER_STEP
    rows = nb * dec_seq
    flat_spec = lambda w: pl.BlockSpec((rows, w), lambda i: (i, 0))
    cache_spec = pl.BlockSpec((nb, N_HEADS, HEAD_DIM, HIST), lambda i: (i, 0, 0, 0))
    y_s, k_s, v_s, c_s = pl.pallas_call(
        _sample_kernel,
        grid=(dec_batch // nb,),
        in_specs=[flat_spec(D_MODEL), flat_spec(D_PLE), cache_spec, cache_spec,
                  pl.BlockSpec((nb, CONV_W - 1, D_CONV), lambda i: (i, 0, 0))]
                 + weight_specs + [_const_spec(gvec.shape)],
        out_specs=[flat_spec(D_MODEL), flat_spec(D_ATT), flat_spec(D_ATT),
                   pl.BlockSpec((nb, 8, D_CONV), lambda i: (i, 0, 0))],
        out_shape=[jax.ShapeDtypeStruct((dec_batch * dec_seq, D_MODEL), F32),
                   jax.ShapeDtypeStruct((dec_batch * dec_seq, D_ATT), F32),
                   jax.ShapeDtypeStruct((dec_batch * dec_seq, D_ATT), F32),
                   jax.ShapeDtypeStruct((dec_batch, 8, D_CONV), F32)],
        scratch_shapes=[pltpu.VMEM((nb, GROUP, D_ATT), BF16), pltpu.VMEM((nb, GROUP, D_ATT), BF16),
                        pltpu.VMEM((N_SLABS, SLAB_HEADS * dec_seq, KWIN), F32)],
        compiler_params=pltpu.CompilerParams(dimension_semantics=("arbitrary",),
                                             vmem_limit_bytes=VMEM_LIMIT_BYTES),
        name="sample_layer",
    )(x_sample.reshape(dec_batch * dec_seq, D_MODEL), p_sample[0].reshape(dec_batch * dec_seq, D_PLE),
      jnp.transpose(cache_k[0], (0, 2, 3, 1)), jnp.transpose(cache_v[0], (0, 2, 3, 1)),
      state_conv[0], *weights, gvec)

    return (y_p,
            y_s.reshape(dec_batch, dec_seq, D_MODEL),
            k_p.reshape(1, batch, HIST, N_HEADS, HEAD_DIM),
            v_p.reshape(1, batch, HIST, N_HEADS, HEAD_DIM),
            c_p[None, :, 8 - (CONV_W - 1):, :],
            k_s.reshape(1, dec_batch, dec_seq, N_HEADS, HEAD_DIM),
            v_s.reshape(1, dec_batch, dec_seq, N_HEADS, HEAD_DIM),
            c_s[None, :, 8 - (CONV_W - 1):, :])
```

```python
import jax
import jax.numpy as jnp
from jax import lax
from jax.experimental import pallas as pl
from jax.experimental.pallas import tpu as pltpu

D_MODEL = 1024
D_CONV = 512
D_ATT = 512
HEAD_DIM = 64
HEAD_SHIFT = 6
N_HEADS = 8
SLAB_HEADS = 4
N_SLABS = N_HEADS // SLAB_HEADS
LANES = 128
SLAB_W = SLAB_HEADS * HEAD_DIM
CHUNK = 64
N_PAST_CHUNKS = 8
HIST = N_PAST_CHUNKS * CHUNK
REL_CLIP = 128
D_PLE = 256
CONV_W = 3
EPS = 1e-6
NEG = -1e30
SCALE = HEAD_DIM ** -0.5

GROUP = 2 * CHUNK
KWIN = HIST + GROUP
KBLOCKS = KWIN // LANES
FLAT = KWIN - REL_CLIP
PROMPT_TILE = 512
HALF = 256
SAMPLE_BATCH_PER_STEP = 4
VMEM_LIMIT_BYTES = 56 * 1024 * 1024

F32 = jnp.float32
BF16 = jnp.bfloat16


def _rms(x, g):
    ms = jnp.mean(x * x, axis=-1, keepdims=True)
    return x * lax.rsqrt(ms + EPS) * g


def _sigmoid(x):
    return 1.0 / (1.0 + jnp.exp(-x))


def _dot(a, b):
    return jnp.dot(a, b, preferred_element_type=F32)


def _run(steps):
    try:
        while True:
            next(steps)
    except StopIteration as done:
        return done.value


def _chain(*stages):
    results = []
    for steps in stages:
        results.append((yield from steps))
    return results


def _zip(main, side, side_per_main):
    side_val, side_done = None, False
    while True:
        try:
            next(main)
        except StopIteration as done:
            main_val = done.value
            break
        for _ in range(side_per_main):
            if not side_done:
                try:
                    next(side)
                except StopIteration as done:
                    side_val, side_done = done.value, True
    if not side_done:
        side_val = _run(side)
    return main_val, side_val


def _proj(xn, win_ref, i):
    return _dot(xn, win_ref[:, 512 * i:512 * (i + 1)])


def _qkv(x, win_ref, nin_ref):
    xn = _rms(x, nin_ref[...]).astype(BF16)
    q = (_proj(xn, win_ref, 4) * SCALE).astype(BF16)
    yield
    k = _proj(xn, win_ref, 5)
    yield
    v = _proj(xn, win_ref, 6)
    return xn, q, k, v


def _conv_gate(xn, win_ref, cw_ref, ncv_ref, prev_rows):
    h = _proj(xn, win_ref, 0)
    yield
    u = _proj(xn, win_ref, 2) * h
    u1, u2 = prev_rows(u)
    cv = cw_ref[0:1, :] * u2 + cw_ref[1:2, :] * u1 + cw_ref[2:3, :] * u
    yield
    zc = _proj(xn, win_ref, 3)
    gz = zc * _sigmoid(zc)
    yield
    yc = _proj(xn, win_ref, 1) * cv * gz
    yc_n = _rms(yc, ncv_ref[...])
    yield
    za = _proj(xn, win_ref, 7)
    return u, yc_n, za


def _dot_t(a, b):
    return lax.dot_general(a, b, (((1,), (1,)), ((), ())), preferred_element_type=F32)


def _attend(members):
    r = members[0][0].shape[0]
    lane = lax.broadcasted_iota(jnp.int32, members[0][0].shape, 1) >> HEAD_SHIFT
    scores = []
    for qslab, keys, _, bias in members:
        zero = jnp.zeros_like(qslab)
        qs = jnp.concatenate([jnp.where(lane == j, qslab, zero) for j in range(SLAB_HEADS)], axis=0)
        parts = [_dot(qs, k) if transposed else _dot_t(qs, k) for k, transposed in keys]
        s = parts[0] if len(parts) == 1 else jnp.concatenate(parts, axis=1)
        scores.append(s + bias)
    s = scores[0] if len(scores) == 1 else jnp.concatenate(scores, axis=0)
    m = jnp.max(s, axis=-1, keepdims=True)
    e = jnp.exp(s - m)
    inv_l = 1.0 / jnp.sum(e, axis=-1, keepdims=True)
    p = e.astype(BF16)
    outs = []
    for i, (_, _, values, _) in enumerate(members):
        rows = slice(i * SLAB_HEADS * r, (i + 1) * SLAB_HEADS * r)
        o, col = None, 0
        for v, transposed in values:
            n = v.shape[1] if transposed else v.shape[0]
            pv = _dot_t(p[rows, col:col + n], v) if transposed else _dot(p[rows, col:col + n], v)
            o = pv if o is None else o + pv
            col += n
        o = o * inv_l[rows]
        out = o[(SLAB_HEADS - 1) * r:]
        for j in reversed(range(SLAB_HEADS - 1)):
            out = jnp.where(lane <= j, o[j * r:(j + 1) * r], out)
        outs.append(out)
    return outs


def _build_bias(gvec_ref, n_rows, in_band, store):
    var0 = FLAT - GROUP
    width = KWIN - var0 + GROUP
    i = lax.broadcasted_iota(jnp.int32, (n_rows, width), 0)
    band = in_band(lax.broadcasted_iota(jnp.int32, (n_rows, KWIN), 0),
                   lax.broadcasted_iota(jnp.int32, (n_rows, KWIN), 1))
    for h in range(N_HEADS):
        x = jnp.broadcast_to(gvec_ref[h:h + 1, var0:], (n_rows, width))
        bit = 0
        while (1 << bit) < n_rows:
            x = jnp.where(((i >> bit) & 1) == 1, pltpu.roll(x, 1 << bit, 1), x)
            bit += 1
        flat = jnp.broadcast_to(gvec_ref[h:h + 1, 0:var0], (n_rows, var0))
        store(h, jnp.where(band, jnp.concatenate([flat, x[:, GROUP:]], axis=1), NEG))


def _tail(x, yc_n, ya, za, p, wout_ref, wg_ref, wp_ref, nat_ref, npl_ref, nfin_ref):
    pe = _dot(p.astype(BF16), wp_ref[...])
    ya_n = _rms(ya * (za * _sigmoid(za)), nat_ref[...])
    yield
    y = _dot(yc_n.astype(BF16), wout_ref[0:D_CONV, :]) + _dot(ya_n.astype(BF16), wout_ref[D_CONV:, :])
    x1 = x + y
    yield
    gate = _sigmoid(_dot(_rms(x1, npl_ref[...]).astype(BF16), wg_ref[...]))
    x2 = x1 + gate * pe
    return _rms(x2, nfin_ref[...])


def _prompt_kernel(x_ref, p_ref, win_ref, wout_ref, wg_ref, wp_ref, nin_ref, cw_ref, ncv_ref,
                   nat_ref, npl_ref, nfin_ref, gvec_ref,
                   y_ref, ks_ref, vs_ref, cs_ref, kbuf, vbuf, ucar, unext, bias_ref):
    t = pl.program_id(1)
    tile = x_ref.shape[0]

    @pl.when((pl.program_id(0) == 0) & (t == 0))
    def _():
        def store(h, tb):
            j = h % SLAB_HEADS
            for cb in range(KBLOCKS):
                bias_ref[h // SLAB_HEADS, cb, j * GROUP:(j + 1) * GROUP, :] = tb[:, cb * LANES:(cb + 1) * LANES]

        _build_bias(gvec_ref, GROUP,
                    lambda i, s: ((i < CHUNK) & (s < HIST + CHUNK)) | ((i >= CHUNK) & (s >= CHUNK)), store)
        bias_ref[N_SLABS] = jnp.full((KBLOCKS, SLAB_HEADS * GROUP, LANES), NEG, F32)

    @pl.when(t == 0)
    def _():
        kbuf[0:HIST, :] = jnp.zeros((HIST, D_ATT), BF16)
        vbuf[0:HIST, :] = jnp.zeros((HIST, D_ATT), BF16)
        ucar[...] = jnp.zeros(ucar.shape, F32)

    @pl.when(t > 0)
    def _():
        kbuf[0:HIST, :] = kbuf[tile:, :]
        vbuf[0:HIST, :] = vbuf[tile:, :]
        ucar[...] = unext[...]

    n_half = tile // HALF
    row = lax.broadcasted_iota(jnp.int32, (HALF, D_CONV), 0)

    def qkv(h):
        rows = slice(h * HALF, (h + 1) * HALF)
        xn, q, k, v = yield from _qkv(x_ref[rows, :], win_ref, nin_ref)
        kbuf[HIST + h * HALF:HIST + (h + 1) * HALF, :] = k.astype(BF16)
        vbuf[HIST + h * HALF:HIST + (h + 1) * HALF, :] = v.astype(BF16)
        ks_ref[rows, :] = k
        vs_ref[rows, :] = v
        return xn, q

    def conv_gate(xn, c1, c2):
        def prev_rows(u):
            u1 = jnp.where(row == 0, c1, pltpu.roll(u, 1, 0))
            u2 = jnp.where(row == 0, c2, jnp.where(row == 1, c1, pltpu.roll(u, 2, 0)))
            return u1, u2

        return _conv_gate(xn, win_ref, cw_ref, ncv_ref, prev_rows)

    def attention(h, q):
        groups = []
        for gl in range(HALF // GROUP):
            g = h * (HALF // GROUP) + gl
            first_key = t * tile + g * GROUP - HIST
            slabs = []
            for p in range(N_SLABS):
                lanes = slice(p * SLAB_W, (p + 1) * SLAB_W)
                bias = jnp.concatenate(
                    [bias_ref[jnp.where(first_key + cb * LANES < 0, N_SLABS, p), cb] for cb in range(KBLOCKS)],
                    axis=1)
                slabs.extend(_attend([(q[gl * GROUP:(gl + 1) * GROUP, lanes],
                                       [(kbuf[g * GROUP:g * GROUP + KWIN, lanes], False)],
                                       [(vbuf[g * GROUP:g * GROUP + KWIN, lanes], False)], bias)]))
                yield
            groups.append(jnp.concatenate(slabs, axis=1))
        return jnp.concatenate(groups, axis=0)

    def tail(h, yc_n, ya, za):
        rows = slice(h * HALF, (h + 1) * HALF)
        y_ref[rows, :] = yield from _tail(x_ref[rows, :], yc_n, ya, za, p_ref[rows, :],
                                          wout_ref, wg_ref, wp_ref, nat_ref, npl_ref, nfin_ref)

    xn, q = _run(qkv(0))
    c1, c2 = ucar[7:8, :], ucar[6:7, :]
    done = None
    for h in range(n_half):
        side = [conv_gate(xn, c1, c2)]
        if h + 1 < n_half:
            side.append(qkv(h + 1))
        if done is not None:
            side.append(tail(h - 1, *done))
        ya, results = _zip(attention(h, q), _chain(*side), 2)
        u, yc_n, za = results[0]
        done = (yc_n, ya, za)
        c1, c2 = u[HALF - 1:HALF, :], u[HALF - 2:HALF - 1, :]
        if h + 1 < n_half:
            xn, q = results[1]
    _run(tail(n_half - 1, *done))
    unext[...] = u[HALF - 8:, :]
    cs_ref[...] = u[HALF - 8:, :]


def _sample_kernel(x_ref, p_ref, ck_ref, cv_ref, sc_ref, win_ref, wout_ref, wg_ref, wp_ref, nin_ref,
                   cw_ref, ncv_ref, nat_ref, npl_ref, nfin_ref, gvec_ref,
                   y_ref, ks_ref, vs_ref, cs_ref, knew, vnew, bias_ref):
    nb = ck_ref.shape[0]
    rows = x_ref.shape[0]
    seq = rows // nb

    @pl.when(pl.program_id(0) == 0)
    def _():
        def store(h, tb):
            j = h % SLAB_HEADS
            bias_ref[h // SLAB_HEADS, j * seq:(j + 1) * seq, :] = tb

        _build_bias(gvec_ref, seq, lambda i, s: s < HIST + seq, store)
        knew[:, seq:, :] = jnp.zeros((nb, GROUP - seq, D_ATT), BF16)
        vnew[:, seq:, :] = jnp.zeros((nb, GROUP - seq, D_ATT), BF16)

    x = x_ref[...]
    rmod = lax.broadcasted_iota(jnp.int32, (rows, D_CONV), 0) % seq

    def prev_rows(u):
        e1 = jnp.concatenate([jnp.broadcast_to(sc_ref[j, 1:2, :], (seq, D_CONV)) for j in range(nb)], axis=0)
        e2 = jnp.concatenate([jnp.broadcast_to(sc_ref[j, 0:1, :], (seq, D_CONV)) for j in range(nb)], axis=0)
        u1 = jnp.where(rmod == 0, e1, pltpu.roll(u, 1, 0))
        u2 = jnp.where(rmod == 0, e2, jnp.where(rmod == 1, e1, pltpu.roll(u, 2, 0)))
        return u1, u2

    xn, q, k, v = _run(_qkv(x, win_ref, nin_ref))
    ks_ref[...] = k
    vs_ref[...] = v
    kb = k.astype(BF16)
    vb = v.astype(BF16)

    def attention():
        cache_kt, cache_vt = [], []
        for j in range(nb):
            rs = slice(j * seq, (j + 1) * seq)
            knew[j, 0:seq, :] = kb[rs, :]
            vnew[j, 0:seq, :] = vb[rs, :]
            cache_kt.append(ck_ref[j].reshape(D_ATT, HIST).astype(BF16))
            cache_vt.append(cv_ref[j].reshape(D_ATT, HIST).astype(BF16))
        slabs = []
        for p in range(N_SLABS):
            lanes = slice(p * SLAB_W, (p + 1) * SLAB_W)
            outs = _attend([(q[j * seq:(j + 1) * seq, lanes],
                             [(cache_kt[j][lanes, :], True), (knew[j, :, lanes], False)],
                             [(cache_vt[j][lanes, :], True), (vnew[j, :, lanes], False)], bias_ref[p])
                            for j in range(nb)])
            slabs.append(jnp.concatenate(outs, axis=0))
            yield
        return jnp.concatenate(slabs, axis=1)

    ya, (u, yc_n, za) = _zip(attention(), _conv_gate(xn, win_ref, cw_ref, ncv_ref, prev_rows), 3)
    for j in range(nb):
        cs_ref[j] = u[(j + 1) * seq - 8:(j + 1) * seq, :]
    y_ref[...] = _run(_tail(x, yc_n, ya, za, p_ref[...], wout_ref, wg_ref, wp_ref, nat_ref, npl_ref, nfin_ref))


def _bias_vector(rel_bias):
    width = KWIN + GROUP
    assert width - FLAT == 2 * REL_CLIP
    return jnp.concatenate([jnp.broadcast_to(rel_bias[:, 2 * REL_CLIP:], (N_HEADS, FLAT)),
                            rel_bias[:, ::-1][:, :2 * REL_CLIP]], axis=1)


def _const_spec(shape):
    nd = len(shape)
    return pl.BlockSpec(shape, lambda *_: (0,) * nd, pipeline_mode=pl.Buffered(1))


def kernel(x_prompt, x_sample, cache_k, cache_v, state_conv, p_prompt, p_sample, norm_in, w_in, conv_w,
           rel_bias, norm_conv, norm_att, w_out, ple_norm, w_ple_gate, w_ple_proj, final_norm):
    depth = w_in.shape[0]
    assert depth == 1, "single-layer kernel"
    batch, seq, _ = x_prompt.shape
    dec_batch, dec_seq, _ = x_sample.shape
    kv_win = cache_k.shape[2]
    assert kv_win == HIST and PROMPT_TILE == HIST and seq % PROMPT_TILE == 0
    assert PROMPT_TILE % HALF == 0 and HALF % GROUP == 0 and (1 << HEAD_SHIFT) == HEAD_DIM
    assert dec_seq % 8 == 0 and dec_seq <= GROUP and dec_batch % SAMPLE_BATCH_PER_STEP == 0

    win = w_in[0].astype(BF16)
    wout = w_out[0].astype(BF16)
    wg = w_ple_gate[0].astype(BF16)
    wp = w_ple_proj[0].astype(BF16)
    nin = norm_in[0][None, :]
    ncv = norm_conv[0][None, :]
    nat = norm_att[0][None, :]
    npl = ple_norm[0][None, :]
    nfin = final_norm[None, :]
    cw = conv_w[0]
    gvec = _bias_vector(rel_bias[0])

    weight_specs = [
        _const_spec(win.shape), _const_spec(wout.shape), _const_spec(wg.shape), _const_spec(wp.shape),
        _const_spec(nin.shape), _const_spec(cw.shape), _const_spec(ncv.shape), _const_spec(nat.shape),
        _const_spec(npl.shape), _const_spec(nfin.shape),
    ]
    weights = (win, wout, wg, wp, nin, cw, ncv, nat, npl, nfin)

    n_t = seq // PROMPT_TILE
    row_spec = lambda w: pl.BlockSpec((None, PROMPT_TILE, w), lambda b, t: (b, t, 0))
    state_spec = pl.BlockSpec((None, HIST, D_ATT), lambda b, t: (b, 0, 0))
    y_p, k_p, v_p, c_p = pl.pallas_call(
        _prompt_kernel,
        grid=(batch, n_t),
        in_specs=[row_spec(D_MODEL), row_spec(D_PLE)] + weight_specs + [_const_spec(gvec.shape)],
        out_specs=[row_spec(D_MODEL), state_spec, state_spec,
                   pl.BlockSpec((None, 8, D_CONV), lambda b, t: (b, 0, 0))],
        out_shape=[jax.ShapeDtypeStruct((batch, seq, D_MODEL), F32),
                   jax.ShapeDtypeStruct((batch, HIST, D_ATT), F32),
                   jax.ShapeDtypeStruct((batch, HIST, D_ATT), F32),
                   jax.ShapeDtypeStruct((batch, 8, D_CONV), F32)],
        scratch_shapes=[pltpu.VMEM((HIST + PROMPT_TILE, D_ATT), BF16),
                        pltpu.VMEM((HIST + PROMPT_TILE, D_ATT), BF16),
                        pltpu.VMEM((8, D_CONV), F32),
                        pltpu.VMEM((8, D_CONV), F32),
                        pltpu.VMEM((N_SLABS + 1, KBLOCKS, SLAB_HEADS * GROUP, LANES), F32)],
        compiler_params=pltpu.CompilerParams(dimension_semantics=("arbitrary", "arbitrary"),
                                             vmem_limit_bytes=VMEM_LIMIT_BYTES),
        name="prompt_layer",
    )(x_prompt, p_prompt[0], *weights, gvec)

    nb = SAMPLE_BATCH_PER_STEP
    rows = nb * dec_seq
    flat_spec = lambda w: pl.BlockSpec((rows, w), lambda i: (i, 0))
    cache_spec = pl.BlockSpec((nb, N_HEADS, HEAD_DIM, HIST), lambda i: (i, 0, 0, 0))
    y_s, k_s, v_s, c_s = pl.pallas_call(
        _sample_kernel,
        grid=(dec_batch // nb,),
        in_specs=[flat_spec(D_MODEL), flat_spec(D_PLE), cache_spec, cache_spec,
                  pl.BlockSpec((nb, CONV_W - 1, D_CONV), lambda i: (i, 0, 0))]
                 + weight_specs + [_const_spec(gvec.shape)],
        out_specs=[flat_spec(D_MODEL), flat_spec(D_ATT), flat_spec(D_ATT),
                   pl.BlockSpec((nb, 8, D_CONV), lambda i: (i, 0, 0))],
        out_shape=[jax.ShapeDtypeStruct((dec_batch * dec_seq, D_MODEL), F32),
                   jax.ShapeDtypeStruct((dec_batch * dec_seq, D_ATT), F32),
                   jax.ShapeDtypeStruct((dec_batch * dec_seq, D_ATT), F32),
                   jax.ShapeDtypeStruct((dec_batch, 8, D_CONV), F32)],
        scratch_shapes=[pltpu.VMEM((nb, GROUP, D_ATT), BF16), pltpu.VMEM((nb, GROUP, D_ATT), BF16),
                        pltpu.VMEM((N_SLABS, SLAB_HEADS * dec_seq, KWIN), F32)],
        compiler_params=pltpu.CompilerParams(dimension_semantics=("arbitrary",),
                                             vmem_limit_bytes=VMEM_LIMIT_BYTES),
        name="sample_layer",
    )(x_sample.reshape(dec_batch * dec_seq, D_MODEL), p_sample[0].reshape(dec_batch * dec_seq, D_PLE),
      jnp.transpose(cache_k[0], (0, 2, 3, 1)), jnp.transpose(cache_v[0], (0, 2, 3, 1)),
      state_conv[0], *weights, gvec)

    return (y_p,
            y_s.reshape(dec_batch, dec_seq, D_MODEL),
            k_p.reshape(1, batch, HIST, N_HEADS, HEAD_DIM),
            v_p.reshape(1, batch, HIST, N_HEADS, HEAD_DIM),
            c_p[None, :, 8 - (CONV_W - 1):, :],
            k_s.reshape(1, dec_batch, dec_seq, N_HEADS, HEAD_DIM),
            v_s.reshape(1, dec_batch, dec_seq, N_HEADS, HEAD_DIM),
            c_s[None, :, 8 - (CONV_W - 1):, :])
```

```python
import jax
import jax.numpy as jnp
from jax import lax
from jax.experimental import pallas as pl
from jax.experimental.pallas import tpu as pltpu

D_MODEL = 1024
D_CONV = 512
D_ATT = 512
HEAD_DIM = 64
HEAD_SHIFT = 6
N_HEADS = 8
SLAB_HEADS = 4
N_SLABS = N_HEADS // SLAB_HEADS
LANES = 128
SLAB_W = SLAB_HEADS * HEAD_DIM
CHUNK = 64
N_PAST_CHUNKS = 8
HIST = N_PAST_CHUNKS * CHUNK
REL_CLIP = 128
D_PLE = 256
CONV_W = 3
EPS = 1e-6
NEG = -1e30
SCALE = HEAD_DIM ** -0.5
SECTION_W = 512
SEC_H, SEC_B, SEC_C, SEC_ZC, SEC_Q, SEC_K, SEC_V, SEC_ZA = range(8)

GROUP = 2 * CHUNK
KWIN = HIST + GROUP
KBLOCKS = KWIN // LANES
FLAT = KWIN - REL_CLIP
PROMPT_TILE = 512
HALF = 256
SAMPLE_BATCH_PER_STEP = 4
VMEM_LIMIT_BYTES = 56 * 1024 * 1024

F32 = jnp.float32
BF16 = jnp.bfloat16


def _rms(x, g):
    ms = jnp.mean(x * x, axis=-1, keepdims=True)
    return x * lax.rsqrt(ms + EPS) * g


def _sigmoid(x):
    return 1.0 / (1.0 + jnp.exp(-x))


def _dot(a, b):
    return jnp.dot(a, b, preferred_element_type=F32)


def _run(steps):
    try:
        while True:
            next(steps)
    except StopIteration as done:
        return done.value


def _chain(*stages):
    results = []
    for steps in stages:
        results.append((yield from steps))
    return results


def _zip(main, side, side_per_main):
    side_val, side_done = None, False
    while True:
        try:
            next(main)
        except StopIteration as done:
            main_val = done.value
            break
        for _ in range(side_per_main):
            if not side_done:
                try:
                    next(side)
                except StopIteration as done:
                    side_val, side_done = done.value, True
    if not side_done:
        side_val = _run(side)
    return main_val, side_val


def _proj(xn, win_ref, section):
    return _dot(xn, win_ref[:, SECTION_W * section:SECTION_W * (section + 1)])


def _qkv(x, win_ref, nin_ref):
    xn = _rms(x, nin_ref[...]).astype(BF16)
    q = (_proj(xn, win_ref, SEC_Q) * SCALE).astype(BF16)
    yield
    k = _proj(xn, win_ref, SEC_K)
    yield
    v = _proj(xn, win_ref, SEC_V)
    return xn, q, k, v


def _conv_gate(xn, win_ref, cw_ref, ncv_ref, prev_rows):
    h = _proj(xn, win_ref, SEC_H)
    yield
    u = _proj(xn, win_ref, SEC_C) * h
    u1, u2 = prev_rows(u)
    cv = cw_ref[0:1, :] * u2 + cw_ref[1:2, :] * u1 + cw_ref[2:3, :] * u
    yield
    zc = _proj(xn, win_ref, SEC_ZC)
    gz = zc * _sigmoid(zc)
    yield
    yc = _proj(xn, win_ref, SEC_B) * cv * gz
    yc_n = _rms(yc, ncv_ref[...])
    yield
    za = _proj(xn, win_ref, SEC_ZA)
    return u, yc_n, za


def _dot_t(a, b):
    return lax.dot_general(a, b, (((1,), (1,)), ((), ())), preferred_element_type=F32)


def _attend(members):
    r = members[0][0].shape[0]
    lane = lax.broadcasted_iota(jnp.int32, members[0][0].shape, 1) >> HEAD_SHIFT
    scores = []
    for qslab, keys, _, bias in members:
        zero = jnp.zeros_like(qslab)
        qs = jnp.concatenate([jnp.where(lane == j, qslab, zero) for j in range(SLAB_HEADS)], axis=0)
        parts = [_dot(qs, k) if transposed else _dot_t(qs, k) for k, transposed in keys]
        s = parts[0] if len(parts) == 1 else jnp.concatenate(parts, axis=1)
        scores.append(s + bias)
    s = scores[0] if len(scores) == 1 else jnp.concatenate(scores, axis=0)
    m = jnp.max(s, axis=-1, keepdims=True)
    e = jnp.exp(s - m)
    inv_l = 1.0 / jnp.sum(e, axis=-1, keepdims=True)
    p = e.astype(BF16)
    outs = []
    for i, (_, _, values, _) in enumerate(members):
        rows = slice(i * SLAB_HEADS * r, (i + 1) * SLAB_HEADS * r)
        o, col = None, 0
        for v, transposed in values:
            n = v.shape[1] if transposed else v.shape[0]
            pv = _dot_t(p[rows, col:col + n], v) if transposed else _dot(p[rows, col:col + n], v)
            o = pv if o is None else o + pv
            col += n
        o = o * inv_l[rows]
        out = o[(SLAB_HEADS - 1) * r:]
        for j in reversed(range(SLAB_HEADS - 1)):
            out = jnp.where(lane <= j, o[j * r:(j + 1) * r], out)
        outs.append(out)
    return outs


def _build_bias(gvec_ref, n_rows, in_band, store):
    var0 = FLAT - GROUP
    width = KWIN - var0 + GROUP
    i = lax.broadcasted_iota(jnp.int32, (n_rows, width), 0)
    band = in_band(lax.broadcasted_iota(jnp.int32, (n_rows, KWIN), 0),
                   lax.broadcasted_iota(jnp.int32, (n_rows, KWIN), 1))
    for h in range(N_HEADS):
        x = jnp.broadcast_to(gvec_ref[h:h + 1, var0:], (n_rows, width))
        bit = 0
        while (1 << bit) < n_rows:
            x = jnp.where(((i >> bit) & 1) == 1, pltpu.roll(x, 1 << bit, 1), x)
            bit += 1
        flat = jnp.broadcast_to(gvec_ref[h:h + 1, 0:var0], (n_rows, var0))
        store(h, jnp.where(band, jnp.concatenate([flat, x[:, GROUP:]], axis=1), NEG))


def _tail(x, yc_n, ya, za, p, wout_ref, wg_ref, wp_ref, nat_ref, npl_ref, nfin_ref):
    pe = _dot(p.astype(BF16), wp_ref[...])
    ya_n = _rms(ya * (za * _sigmoid(za)), nat_ref[...])
    yield
    y = _dot(yc_n.astype(BF16), wout_ref[0:D_CONV, :]) + _dot(ya_n.astype(BF16), wout_ref[D_CONV:, :])
    x1 = x + y
    yield
    gate = _sigmoid(_dot(_rms(x1, npl_ref[...]).astype(BF16), wg_ref[...]))
    x2 = x1 + gate * pe
    return _rms(x2, nfin_ref[...])


def _prompt_kernel(x_ref, p_ref, win_ref, wout_ref, wg_ref, wp_ref, nin_ref, cw_ref, ncv_ref,
                   nat_ref, npl_ref, nfin_ref, gvec_ref,
                   y_ref, ks_ref, vs_ref, cs_ref, kbuf, vbuf, ucar, unext, bias_ref):
    t = pl.program_id(1)
    tile = x_ref.shape[0]

    @pl.when((pl.program_id(0) == 0) & (t == 0))
    def _():
        def store(h, tb):
            j = h % SLAB_HEADS
            for cb in range(KBLOCKS):
                bias_ref[h // SLAB_HEADS, cb, j * GROUP:(j + 1) * GROUP, :] = tb[:, cb * LANES:(cb + 1) * LANES]

        _build_bias(gvec_ref, GROUP,
                    lambda i, s: ((i < CHUNK) & (s < HIST + CHUNK)) | ((i >= CHUNK) & (s >= CHUNK)), store)
        bias_ref[N_SLABS] = jnp.full((KBLOCKS, SLAB_HEADS * GROUP, LANES), NEG, F32)

    @pl.when(t == 0)
    def _():
        kbuf[0:HIST, :] = jnp.zeros((HIST, D_ATT), BF16)
        vbuf[0:HIST, :] = jnp.zeros((HIST, D_ATT), BF16)
        ucar[...] = jnp.zeros(ucar.shape, F32)

    @pl.when(t > 0)
    def _():
        kbuf[0:HIST, :] = kbuf[tile:, :]
        vbuf[0:HIST, :] = vbuf[tile:, :]
        ucar[...] = unext[...]

    n_half = tile // HALF
    row = lax.broadcasted_iota(jnp.int32, (HALF, D_CONV), 0)

    def qkv(h):
        rows = slice(h * HALF, (h + 1) * HALF)
        xn, q, k, v = yield from _qkv(x_ref[rows, :], win_ref, nin_ref)
        kbuf[HIST + h * HALF:HIST + (h + 1) * HALF, :] = k.astype(BF16)
        vbuf[HIST + h * HALF:HIST + (h + 1) * HALF, :] = v.astype(BF16)
        ks_ref[rows, :] = k
        vs_ref[rows, :] = v
        return xn, q

    def conv_gate(xn, c1, c2):
        def prev_rows(u):
            u1 = jnp.where(row == 0, c1, pltpu.roll(u, 1, 0))
            u2 = jnp.where(row == 0, c2, jnp.where(row == 1, c1, pltpu.roll(u, 2, 0)))
            return u1, u2

        return _conv_gate(xn, win_ref, cw_ref, ncv_ref, prev_rows)

    def attention(h, q):
        groups = []
        for gl in range(HALF // GROUP):
            g = h * (HALF // GROUP) + gl
            first_key = t * tile + g * GROUP - HIST
            slabs = []
            for p in range(N_SLABS):
                lanes = slice(p * SLAB_W, (p + 1) * SLAB_W)
                bias = jnp.concatenate(
                    [bias_ref[jnp.where(first_key + cb * LANES < 0, N_SLABS, p), cb] for cb in range(KBLOCKS)],
                    axis=1)
                slabs.extend(_attend([(q[gl * GROUP:(gl + 1) * GROUP, lanes],
                                       [(kbuf[g * GROUP:g * GROUP + KWIN, lanes], False)],
                                       [(vbuf[g * GROUP:g * GROUP + KWIN, lanes], False)], bias)]))
                yield
            groups.append(jnp.concatenate(slabs, axis=1))
        return jnp.concatenate(groups, axis=0)

    def tail(h, yc_n, ya, za):
        rows = slice(h * HALF, (h + 1) * HALF)
        y_ref[rows, :] = yield from _tail(x_ref[rows, :], yc_n, ya, za, p_ref[rows, :],
                                          wout_ref, wg_ref, wp_ref, nat_ref, npl_ref, nfin_ref)

    xn, q = _run(qkv(0))
    c1, c2 = ucar[7:8, :], ucar[6:7, :]
    done = None
    for h in range(n_half):
        side = [conv_gate(xn, c1, c2)]
        if h + 1 < n_half:
            side.append(qkv(h + 1))
        if done is not None:
            side.append(tail(h - 1, *done))
        ya, results = _zip(attention(h, q), _chain(*side), 2)
        u, yc_n, za = results[0]
        done = (yc_n, ya, za)
        c1, c2 = u[HALF - 1:HALF, :], u[HALF - 2:HALF - 1, :]
        if h + 1 < n_half:
            xn, q = results[1]
    _run(tail(n_half - 1, *done))
    unext[...] = u[HALF - 8:, :]
    cs_ref[...] = u[HALF - 8:, :]


def _sample_kernel(x_ref, p_ref, ck_ref, cv_ref, sc_ref, win_ref, wout_ref, wg_ref, wp_ref, nin_ref,
                   cw_ref, ncv_ref, nat_ref, npl_ref, nfin_ref, gvec_ref,
                   y_ref, ks_ref, vs_ref, cs_ref, knew, vnew, bias_ref):
    nb = ck_ref.shape[0]
    rows = x_ref.shape[0]
    seq = rows // nb

    @pl.when(pl.program_id(0) == 0)
    def _():
        def store(h, tb):
            j = h % SLAB_HEADS
            bias_ref[h // SLAB_HEADS, j * seq:(j + 1) * seq, :] = tb

        _build_bias(gvec_ref, seq, lambda i, s: s < HIST + seq, store)
        knew[:, seq:, :] = jnp.zeros((nb, GROUP - seq, D_ATT), BF16)
        vnew[:, seq:, :] = jnp.zeros((nb, GROUP - seq, D_ATT), BF16)

    x = x_ref[...]
    rmod = lax.broadcasted_iota(jnp.int32, (rows, D_CONV), 0) % seq

    def prev_rows(u):
        e1 = jnp.concatenate([jnp.broadcast_to(sc_ref[j, 1:2, :], (seq, D_CONV)) for j in range(nb)], axis=0)
        e2 = jnp.concatenate([jnp.broadcast_to(sc_ref[j, 0:1, :], (seq, D_CONV)) for j in range(nb)], axis=0)
        u1 = jnp.where(rmod == 0, e1, pltpu.roll(u, 1, 0))
        u2 = jnp.where(rmod == 0, e2, jnp.where(rmod == 1, e1, pltpu.roll(u, 2, 0)))
        return u1, u2

    xn, q, k, v = _run(_qkv(x, win_ref, nin_ref))
    ks_ref[...] = k
    vs_ref[...] = v
    kb = k.astype(BF16)
    vb = v.astype(BF16)

    def attention():
        cache_kt, cache_vt = [], []
        for j in range(nb):
            rs = slice(j * seq, (j + 1) * seq)
            knew[j, 0:seq, :] = kb[rs, :]
            vnew[j, 0:seq, :] = vb[rs, :]
            cache_kt.append(ck_ref[j].reshape(D_ATT, HIST).astype(BF16))
            cache_vt.append(cv_ref[j].reshape(D_ATT, HIST).astype(BF16))
        slabs = []
        for p in range(N_SLABS):
            lanes = slice(p * SLAB_W, (p + 1) * SLAB_W)
            outs = _attend([(q[j * seq:(j + 1) * seq, lanes],
                             [(cache_kt[j][lanes, :], True), (knew[j, :, lanes], False)],
                             [(cache_vt[j][lanes, :], True), (vnew[j, :, lanes], False)], bias_ref[p])
                            for j in range(nb)])
            slabs.append(jnp.concatenate(outs, axis=0))
            yield
        return jnp.concatenate(slabs, axis=1)

    ya, (u, yc_n, za) = _zip(attention(), _conv_gate(xn, win_ref, cw_ref, ncv_ref, prev_rows), 3)
    for j in range(nb):
        cs_ref[j] = u[(j + 1) * seq - 8:(j + 1) * seq, :]
    y_ref[...] = _run(_tail(x, yc_n, ya, za, p_ref[...], wout_ref, wg_ref, wp_ref, nat_ref, npl_ref, nfin_ref))


def _bias_vector(rel_bias):
    width = KWIN + GROUP
    assert width - FLAT == 2 * REL_CLIP
    return jnp.concatenate([jnp.broadcast_to(rel_bias[:, 2 * REL_CLIP:], (N_HEADS, FLAT)),
                            rel_bias[:, ::-1][:, :2 * REL_CLIP]], axis=1)


def _const_spec(shape):
    nd = len(shape)
    return pl.BlockSpec(shape, lambda *_: (0,) * nd, pipeline_mode=pl.Buffered(1))


def kernel(x_prompt, x_sample, cache_k, cache_v, state_conv, p_prompt, p_sample, norm_in, w_in, conv_w,
           rel_bias, norm_conv, norm_att, w_out, ple_norm, w_ple_gate, w_ple_proj, final_norm):
    depth = w_in.shape[0]
    assert depth == 1, "single-layer kernel"
    batch, seq, _ = x_prompt.shape
    dec_batch, dec_seq, _ = x_sample.shape
    kv_win = cache_k.shape[2]
    assert kv_win == HIST and PROMPT_TILE == HIST and seq % PROMPT_TILE == 0
    assert PROMPT_TILE % HALF == 0 and HALF % GROUP == 0 and (1 << HEAD_SHIFT) == HEAD_DIM
    assert SECTION_W == D_CONV == D_ATT and w_in.shape[2] == 8 * SECTION_W
    assert dec_seq % 8 == 0 and dec_seq <= GROUP and dec_batch % SAMPLE_BATCH_PER_STEP == 0

    win = w_in[0].astype(BF16)
    wout = w_out[0].astype(BF16)
    wg = w_ple_gate[0].astype(BF16)
    wp = w_ple_proj[0].astype(BF16)
    nin = norm_in[0][None, :]
    ncv = norm_conv[0][None, :]
    nat = norm_att[0][None, :]
    npl = ple_norm[0][None, :]
    nfin = final_norm[None, :]
    cw = conv_w[0]
    gvec = _bias_vector(rel_bias[0])

    weight_specs = [
        _const_spec(win.shape), _const_spec(wout.shape), _const_spec(wg.shape), _const_spec(wp.shape),
        _const_spec(nin.shape), _const_spec(cw.shape), _const_spec(ncv.shape), _const_spec(nat.shape),
        _const_spec(npl.shape), _const_spec(nfin.shape),
    ]
    weights = (win, wout, wg, wp, nin, cw, ncv, nat, npl, nfin)

    n_t = seq // PROMPT_TILE
    row_spec = lambda w: pl.BlockSpec((None, PROMPT_TILE, w), lambda b, t: (b, t, 0))
    state_spec = pl.BlockSpec((None, HIST, D_ATT), lambda b, t: (b, 0, 0))
    y_p, k_p, v_p, c_p = pl.pallas_call(
        _prompt_kernel,
        grid=(batch, n_t),
        in_specs=[row_spec(D_MODEL), row_spec(D_PLE)] + weight_specs + [_const_spec(gvec.shape)],
        out_specs=[row_spec(D_MODEL), state_spec, state_spec,
                   pl.BlockSpec((None, 8, D_CONV), lambda b, t: (b, 0, 0))],
        out_shape=[jax.ShapeDtypeStruct((batch, seq, D_MODEL), F32),
                   jax.ShapeDtypeStruct((batch, HIST, D_ATT), F32),
                   jax.ShapeDtypeStruct((batch, HIST, D_ATT), F32),
                   jax.ShapeDtypeStruct((batch, 8, D_CONV), F32)],
        scratch_shapes=[pltpu.VMEM((HIST + PROMPT_TILE, D_ATT), BF16),
                        pltpu.VMEM((HIST + PROMPT_TILE, D_ATT), BF16),
                        pltpu.VMEM((8, D_CONV), F32),
                        pltpu.VMEM((8, D_CONV), F32),
                        pltpu.VMEM((N_SLABS + 1, KBLOCKS, SLAB_HEADS * GROUP, LANES), F32)],
        compiler_params=pltpu.CompilerParams(dimension_semantics=("arbitrary", "arbitrary"),
                                             vmem_limit_bytes=VMEM_LIMIT_BYTES),
        name="prompt_layer",
    )(x_prompt, p_prompt[0], *weights, gvec)

    nb = SAMPLE_BATCH_PER_STEP
    rows = nb * dec_seq
    flat_spec = lambda w: pl.BlockSpec((rows, w), lambda i: (i, 0))
    cache_spec = pl.BlockSpec((nb, N_HEADS, HEAD_DIM, HIST), lambda i: (i, 0, 0, 0))
    y_s, k_s, v_s, c_s = pl.pallas_call(
        _sample_kernel,
        grid=(dec_batch // nb,),
        in_specs=[flat_spec(D_MODEL), flat_spec(D_PLE), cache_spec, cache_spec,
                  pl.BlockSpec((nb, CONV_W - 1, D_CONV), lambda i: (i, 0, 0))]
                 + weight_specs + [_const_spec(gvec.shape)],
        out_specs=[flat_spec(D_MODEL), flat_spec(D_ATT), flat_spec(D_ATT),
                   pl.BlockSpec((nb, 8, D_CONV), lambda i: (i, 0, 0))],
        out_shape=[jax.ShapeDtypeStruct((dec_batch * dec_seq, D_MODEL), F32),
                   jax.ShapeDtypeStruct((dec_batch * dec_seq, D_ATT), F32),
                   jax.ShapeDtypeStruct((dec_batch * dec_seq, D_ATT), F32),
                   jax.ShapeDtypeStruct((dec_batch, 8, D_CONV), F32)],
        scratch_shapes=[pltpu.VMEM((nb, GROUP, D_ATT), BF16), pltpu.VMEM((nb, GROUP, D_ATT), BF16),
                        pltpu.VMEM((N_SLABS, SLAB_HEADS * dec_seq, KWIN), F32)],
        compiler_params=pltpu.CompilerParams(dimension_semantics=("arbitrary",),
                                             vmem_limit_bytes=VMEM_LIMIT_BYTES),
        name="sample_layer",
    )(x_sample.reshape(dec_batch * dec_seq, D_MODEL), p_sample[0].reshape(dec_batch * dec_seq, D_PLE),
      jnp.transpose(cache_k[0], (0, 2, 3, 1)), jnp.transpose(cache_v[0], (0, 2, 3, 1)),
      state_conv[0], *weights, gvec)

    return (y_p,
            y_s.reshape(dec_batch, dec_seq, D_MODEL),
            k_p.reshape(1, batch, HIST, N_HEADS, HEAD_DIM),
            v_p.reshape(1, batch, HIST, N_HEADS, HEAD_DIM),
            c_p[None, :, 8 - (CONV_W - 1):, :],
            k_s.reshape(1, dec_batch, dec_seq, N_HEADS, HEAD_DIM),
            v_s.reshape(1, dec_batch, dec_seq, N_HEADS, HEAD_DIM),
            c_s[None, :, 8 - (CONV_W - 1):, :])
```

```python
import jax
import jax.numpy as jnp
from jax import lax
from jax.experimental import pallas as pl
from jax.experimental.pallas import tpu as pltpu

D_MODEL = 1024
D_CONV = 512
D_ATT = 512
HEAD_DIM = 64
HEAD_SHIFT = 6
N_HEADS = 8
SLAB_HEADS = 4
N_SLABS = N_HEADS // SLAB_HEADS
LANES = 128
SLAB_W = SLAB_HEADS * HEAD_DIM
CHUNK = 64
N_PAST_CHUNKS = 8
HIST = N_PAST_CHUNKS * CHUNK
REL_CLIP = 128
D_PLE = 256
CONV_W = 3
EPS = 1e-6
NEG = -1e30
SCALE = HEAD_DIM ** -0.5
SECTION_W = 512
SEC_H, SEC_B, SEC_C, SEC_ZC, SEC_Q, SEC_K, SEC_V, SEC_ZA = range(8)

GROUP = 2 * CHUNK
KWIN = HIST + GROUP
KBLOCKS = KWIN // LANES
FLAT = KWIN - REL_CLIP
PROMPT_TILE = 512
HALF = 256
SAMPLE_BATCH_PER_STEP = 4
VMEM_LIMIT_BYTES = 56 * 1024 * 1024

F32 = jnp.float32
BF16 = jnp.bfloat16


def _rms(x, g):
    ms = jnp.mean(x * x, axis=-1, keepdims=True)
    return x * lax.rsqrt(ms + EPS) * g


def _sigmoid(x):
    return 1.0 / (1.0 + jnp.exp(-x))


def _dot(a, b):
    return jnp.dot(a, b, preferred_element_type=F32)


def _run(steps):
    try:
        while True:
            next(steps)
    except StopIteration as done:
        return done.value


def _chain(*stages):
    results = []
    for steps in stages:
        results.append((yield from steps))
    return results


def _zip(main, side, side_per_main):
    side_val, side_done = None, False
    while True:
        try:
            next(main)
        except StopIteration as done:
            main_val = done.value
            break
        for _ in range(side_per_main):
            if not side_done:
                try:
                    next(side)
                except StopIteration as done:
                    side_val, side_done = done.value, True
    if not side_done:
        side_val = _run(side)
    return main_val, side_val


def _proj(xn, win_ref, section):
    return _dot(xn, win_ref[:, SECTION_W * section:SECTION_W * (section + 1)])


def _qkv(x, win_ref, nin_ref):
    xn = _rms(x, nin_ref[...]).astype(BF16)
    q = (_proj(xn, win_ref, SEC_Q) * SCALE).astype(BF16)
    yield
    k = _proj(xn, win_ref, SEC_K)
    yield
    v = _proj(xn, win_ref, SEC_V)
    return xn, q, k, v


def _conv_gate(xn, win_ref, cw_ref, ncv_ref, prev_rows):
    h = _proj(xn, win_ref, SEC_H)
    yield
    u = _proj(xn, win_ref, SEC_C) * h
    u1, u2 = prev_rows(u)
    cv = cw_ref[0:1, :] * u2 + cw_ref[1:2, :] * u1 + cw_ref[2:3, :] * u
    yield
    zc = _proj(xn, win_ref, SEC_ZC)
    gz = zc * _sigmoid(zc)
    yield
    yc = _proj(xn, win_ref, SEC_B) * cv * gz
    yc_n = _rms(yc, ncv_ref[...])
    yield
    za = _proj(xn, win_ref, SEC_ZA)
    return u, yc_n, za


def _dot_t(a, b):
    return lax.dot_general(a, b, (((1,), (1,)), ((), ())), preferred_element_type=F32)


def _attend(members):
    r = members[0][0].shape[0]
    lane = lax.broadcasted_iota(jnp.int32, members[0][0].shape, 1) >> HEAD_SHIFT
    scores = []
    for qslab, keys, _, bias in members:
        zero = jnp.zeros_like(qslab)
        qs = jnp.concatenate([jnp.where(lane == j, qslab, zero) for j in range(SLAB_HEADS)], axis=0)
        parts = [_dot(qs, k) if transposed else _dot_t(qs, k) for k, transposed in keys]
        s = parts[0] if len(parts) == 1 else jnp.concatenate(parts, axis=1)
        scores.append(s + bias)
    s = scores[0] if len(scores) == 1 else jnp.concatenate(scores, axis=0)
    m = jnp.max(s, axis=-1, keepdims=True)
    e = jnp.exp(s - m)
    inv_l = 1.0 / jnp.sum(e, axis=-1, keepdims=True)
    p = e.astype(BF16)
    outs = []
    for i, (_, _, values, _) in enumerate(members):
        rows = slice(i * SLAB_HEADS * r, (i + 1) * SLAB_HEADS * r)
        o, col = None, 0
        for v, transposed in values:
            n = v.shape[1] if transposed else v.shape[0]
            pv = _dot_t(p[rows, col:col + n], v) if transposed else _dot(p[rows, col:col + n], v)
            o = pv if o is None else o + pv
            col += n
        o = o * inv_l[rows]
        out = o[(SLAB_HEADS - 1) * r:]
        for j in reversed(range(SLAB_HEADS - 1)):
            out = jnp.where(lane <= j, o[j * r:(j + 1) * r], out)
        outs.append(out)
    return outs


def _build_bias(gvec_ref, n_rows, in_band, store):
    var0 = FLAT - GROUP
    width = KWIN - var0 + GROUP
    i = lax.broadcasted_iota(jnp.int32, (n_rows, width), 0)
    band = in_band(lax.broadcasted_iota(jnp.int32, (n_rows, KWIN), 0),
                   lax.broadcasted_iota(jnp.int32, (n_rows, KWIN), 1))
    for h in range(N_HEADS):
        x = jnp.broadcast_to(gvec_ref[h:h + 1, var0:], (n_rows, width))
        bit = 0
        while (1 << bit) < n_rows:
            x = jnp.where(((i >> bit) & 1) == 1, pltpu.roll(x, 1 << bit, 1), x)
            bit += 1
        flat = jnp.broadcast_to(gvec_ref[h:h + 1, 0:var0], (n_rows, var0))
        store(h, jnp.where(band, jnp.concatenate([flat, x[:, GROUP:]], axis=1), NEG))


def _tail(x, yc_n, ya, za, p, wout_ref, wg_ref, wp_ref, nat_ref, npl_ref, nfin_ref):
    pe = _dot(p.astype(BF16), wp_ref[...])
    ya_n = _rms(ya * (za * _sigmoid(za)), nat_ref[...])
    yield
    y = _dot(yc_n.astype(BF16), wout_ref[0:D_CONV, :]) + _dot(ya_n.astype(BF16), wout_ref[D_CONV:, :])
    x1 = x + y
    yield
    gate = _sigmoid(_dot(_rms(x1, npl_ref[...]).astype(BF16), wg_ref[...]))
    x2 = x1 + gate * pe
    return _rms(x2, nfin_ref[...])


def _prompt_kernel(x_ref, p_ref, win_ref, wout_ref, wg_ref, wp_ref, nin_ref, cw_ref, ncv_ref,
                   nat_ref, npl_ref, nfin_ref, gvec_ref,
                   y_ref, ks_ref, vs_ref, cs_ref, kbuf, vbuf, ucar, unext, bias_ref):
    t = pl.program_id(1)
    tile = x_ref.shape[0]

    @pl.when((pl.program_id(0) == 0) & (t == 0))
    def _():
        def store(h, tb):
            j = h % SLAB_HEADS
            for cb in range(KBLOCKS):
                bias_ref[h // SLAB_HEADS, cb, j * GROUP:(j + 1) * GROUP, :] = tb[:, cb * LANES:(cb + 1) * LANES]

        _build_bias(gvec_ref, GROUP,
                    lambda i, s: ((i < CHUNK) & (s < HIST + CHUNK)) | ((i >= CHUNK) & (s >= CHUNK)), store)

    @pl.when(t > 0)
    def _():
        kbuf[0:HIST, :] = kbuf[tile:, :]
        vbuf[0:HIST, :] = vbuf[tile:, :]
        ucar[...] = unext[...]

    n_half = tile // HALF
    row = lax.broadcasted_iota(jnp.int32, (HALF, D_CONV), 0)

    def qkv(h):
        rows = slice(h * HALF, (h + 1) * HALF)
        xn, q, k, v = yield from _qkv(x_ref[rows, :], win_ref, nin_ref)
        kbuf[HIST + h * HALF:HIST + (h + 1) * HALF, :] = k.astype(BF16)
        vbuf[HIST + h * HALF:HIST + (h + 1) * HALF, :] = v.astype(BF16)
        ks_ref[rows, :] = k
        vs_ref[rows, :] = v
        return xn, q

    def conv_gate(xn, c1, c2):
        def prev_rows(u):
            u1 = jnp.where(row == 0, c1, pltpu.roll(u, 1, 0))
            u2 = jnp.where(row == 0, c2, jnp.where(row == 1, c1, pltpu.roll(u, 2, 0)))
            return u1, u2

        return _conv_gate(xn, win_ref, cw_ref, ncv_ref, prev_rows)

    def attention(h, q, first_tile):
        groups = []
        for gl in range(HALF // GROUP):
            g = h * (HALF // GROUP) + gl
            cb0 = max(0, (HIST - g * GROUP) // LANES) if first_tile else 0
            key_rows = slice(g * GROUP + cb0 * LANES, g * GROUP + KWIN)
            slabs = []
            for p in range(N_SLABS):
                lanes = slice(p * SLAB_W, (p + 1) * SLAB_W)
                bias = jnp.concatenate([bias_ref[p, cb] for cb in range(cb0, KBLOCKS)], axis=1)
                slabs.extend(_attend([(q[gl * GROUP:(gl + 1) * GROUP, lanes],
                                       [(kbuf[key_rows, lanes], False)], [(vbuf[key_rows, lanes], False)], bias)]))
                yield
            groups.append(jnp.concatenate(slabs, axis=1))
        return jnp.concatenate(groups, axis=0)

    def tail(h, yc_n, ya, za):
        rows = slice(h * HALF, (h + 1) * HALF)
        y_ref[rows, :] = yield from _tail(x_ref[rows, :], yc_n, ya, za, p_ref[rows, :],
                                          wout_ref, wg_ref, wp_ref, nat_ref, npl_ref, nfin_ref)

    def layer(first_tile):
        xn, q = _run(qkv(0))
        if first_tile:
            c1 = c2 = jnp.zeros((1, D_CONV), F32)
        else:
            c1, c2 = ucar[7:8, :], ucar[6:7, :]
        done = None
        for h in range(n_half):
            side = [conv_gate(xn, c1, c2)]
            if h + 1 < n_half:
                side.append(qkv(h + 1))
            if done is not None:
                side.append(tail(h - 1, *done))
            ya, results = _zip(attention(h, q, first_tile), _chain(*side), 2)
            u, yc_n, za = results[0]
            done = (yc_n, ya, za)
            c1, c2 = u[HALF - 1:HALF, :], u[HALF - 2:HALF - 1, :]
            if h + 1 < n_half:
                xn, q = results[1]
        _run(tail(n_half - 1, *done))
        unext[...] = u[HALF - 8:, :]
        cs_ref[...] = u[HALF - 8:, :]

    pl.when(t == 0)(lambda: layer(True))
    pl.when(t > 0)(lambda: layer(False))


def _sample_kernel(x_ref, p_ref, ck_ref, cv_ref, sc_ref, win_ref, wout_ref, wg_ref, wp_ref, nin_ref,
                   cw_ref, ncv_ref, nat_ref, npl_ref, nfin_ref, gvec_ref,
                   y_ref, ks_ref, vs_ref, cs_ref, knew, vnew, bias_ref):
    nb = ck_ref.shape[0]
    rows = x_ref.shape[0]
    seq = rows // nb

    @pl.when(pl.program_id(0) == 0)
    def _():
        def store(h, tb):
            j = h % SLAB_HEADS
            bias_ref[h // SLAB_HEADS, j * seq:(j + 1) * seq, :] = tb

        _build_bias(gvec_ref, seq, lambda i, s: s < HIST + seq, store)
        knew[:, seq:, :] = jnp.zeros((nb, GROUP - seq, D_ATT), BF16)
        vnew[:, seq:, :] = jnp.zeros((nb, GROUP - seq, D_ATT), BF16)

    x = x_ref[...]
    rmod = lax.broadcasted_iota(jnp.int32, (rows, D_CONV), 0) % seq

    def prev_rows(u):
        e1 = jnp.concatenate([jnp.broadcast_to(sc_ref[j, 1:2, :], (seq, D_CONV)) for j in range(nb)], axis=0)
        e2 = jnp.concatenate([jnp.broadcast_to(sc_ref[j, 0:1, :], (seq, D_CONV)) for j in range(nb)], axis=0)
        u1 = jnp.where(rmod == 0, e1, pltpu.roll(u, 1, 0))
        u2 = jnp.where(rmod == 0, e2, jnp.where(rmod == 1, e1, pltpu.roll(u, 2, 0)))
        return u1, u2

    xn, q, k, v = _run(_qkv(x, win_ref, nin_ref))
    ks_ref[...] = k
    vs_ref[...] = v
    kb = k.astype(BF16)
    vb = v.astype(BF16)

    def attention():
        cache_kt, cache_vt = [], []
        for j in range(nb):
            rs = slice(j * seq, (j + 1) * seq)
            knew[j, 0:seq, :] = kb[rs, :]
            vnew[j, 0:seq, :] = vb[rs, :]
            cache_kt.append(ck_ref[j].reshape(D_ATT, HIST).astype(BF16))
            cache_vt.append(cv_ref[j].reshape(D_ATT, HIST).astype(BF16))
        slabs = []
        for p in range(N_SLABS):
            lanes = slice(p * SLAB_W, (p + 1) * SLAB_W)
            outs = _attend([(q[j * seq:(j + 1) * seq, lanes],
                             [(cache_kt[j][lanes, :], True), (knew[j, :, lanes], False)],
                             [(cache_vt[j][lanes, :], True), (vnew[j, :, lanes], False)], bias_ref[p])
                            for j in range(nb)])
            slabs.append(jnp.concatenate(outs, axis=0))
            yield
        return jnp.concatenate(slabs, axis=1)

    ya, (u, yc_n, za) = _zip(attention(), _conv_gate(xn, win_ref, cw_ref, ncv_ref, prev_rows), 3)
    for j in range(nb):
        cs_ref[j] = u[(j + 1) * seq - 8:(j + 1) * seq, :]
    y_ref[...] = _run(_tail(x, yc_n, ya, za, p_ref[...], wout_ref, wg_ref, wp_ref, nat_ref, npl_ref, nfin_ref))


def _bias_vector(rel_bias):
    width = KWIN + GROUP
    assert width - FLAT == 2 * REL_CLIP
    return jnp.concatenate([jnp.broadcast_to(rel_bias[:, 2 * REL_CLIP:], (N_HEADS, FLAT)),
                            rel_bias[:, ::-1][:, :2 * REL_CLIP]], axis=1)


def _const_spec(shape):
    nd = len(shape)
    return pl.BlockSpec(shape, lambda *_: (0,) * nd, pipeline_mode=pl.Buffered(1))


def kernel(x_prompt, x_sample, cache_k, cache_v, state_conv, p_prompt, p_sample, norm_in, w_in, conv_w,
           rel_bias, norm_conv, norm_att, w_out, ple_norm, w_ple_gate, w_ple_proj, final_norm):
    depth = w_in.shape[0]
    assert depth == 1, "single-layer kernel"
    batch, seq, _ = x_prompt.shape
    dec_batch, dec_seq, _ = x_sample.shape
    kv_win = cache_k.shape[2]
    assert kv_win == HIST and PROMPT_TILE == HIST and seq % PROMPT_TILE == 0
    assert PROMPT_TILE % HALF == 0 and HALF % GROUP == 0 and (1 << HEAD_SHIFT) == HEAD_DIM
    assert SECTION_W == D_CONV == D_ATT and w_in.shape[2] == 8 * SECTION_W
    assert dec_seq % 8 == 0 and dec_seq <= GROUP and dec_batch % SAMPLE_BATCH_PER_STEP == 0

    win = w_in[0].astype(BF16)
    wout = w_out[0].astype(BF16)
    wg = w_ple_gate[0].astype(BF16)
    wp = w_ple_proj[0].astype(BF16)
    nin = norm_in[0][None, :]
    ncv = norm_conv[0][None, :]
    nat = norm_att[0][None, :]
    npl = ple_norm[0][None, :]
    nfin = final_norm[None, :]
    cw = conv_w[0]
    gvec = _bias_vector(rel_bias[0])

    weight_specs = [
        _const_spec(win.shape), _const_spec(wout.shape), _const_spec(wg.shape), _const_spec(wp.shape),
        _const_spec(nin.shape), _const_spec(cw.shape), _const_spec(ncv.shape), _const_spec(nat.shape),
        _const_spec(npl.shape), _const_spec(nfin.shape),
    ]
    weights = (win, wout, wg, wp, nin, cw, ncv, nat, npl, nfin)

    n_t = seq // PROMPT_TILE
    row_spec = lambda w: pl.BlockSpec((None, PROMPT_TILE, w), lambda b, t: (b, t, 0))
    state_spec = pl.BlockSpec((None, HIST, D_ATT), lambda b, t: (b, 0, 0))
    y_p, k_p, v_p, c_p = pl.pallas_call(
        _prompt_kernel,
        grid=(batch, n_t),
        in_specs=[row_spec(D_MODEL), row_spec(D_PLE)] + weight_specs + [_const_spec(gvec.shape)],
        out_specs=[row_spec(D_MODEL), state_spec, state_spec,
                   pl.BlockSpec((None, 8, D_CONV), lambda b, t: (b, 0, 0))],
        out_shape=[jax.ShapeDtypeStruct((batch, seq, D_MODEL), F32),
                   jax.ShapeDtypeStruct((batch, HIST, D_ATT), F32),
                   jax.ShapeDtypeStruct((batch, HIST, D_ATT), F32),
                   jax.ShapeDtypeStruct((batch, 8, D_CONV), F32)],
        scratch_shapes=[pltpu.VMEM((HIST + PROMPT_TILE, D_ATT), BF16),
                        pltpu.VMEM((HIST + PROMPT_TILE, D_ATT), BF16),
                        pltpu.VMEM((8, D_CONV), F32),
                        pltpu.VMEM((8, D_CONV), F32),
                        pltpu.VMEM((N_SLABS, KBLOCKS, SLAB_HEADS * GROUP, LANES), F32)],
        compiler_params=pltpu.CompilerParams(dimension_semantics=("arbitrary", "arbitrary"),
                                             vmem_limit_bytes=VMEM_LIMIT_BYTES),
        name="prompt_layer",
    )(x_prompt, p_prompt[0], *weights, gvec)

    nb = SAMPLE_BATCH_PER_STEP
    rows = nb * dec_seq
    flat_spec = lambda w: pl.BlockSpec((rows, w), lambda i: (i, 0))
    cache_spec = pl.BlockSpec((nb, N_HEADS, HEAD_DIM, HIST), lambda i: (i, 0, 0, 0))
    y_s, k_s, v_s, c_s = pl.pallas_call(
        _sample_kernel,
        grid=(dec_batch // nb,),
        in_specs=[flat_spec(D_MODEL), flat_spec(D_PLE), cache_spec, cache_spec,
                  pl.BlockSpec((nb, CONV_W - 1, D_CONV), lambda i: (i, 0, 0))]
                 + weight_specs + [_const_spec(gvec.shape)],
        out_specs=[flat_spec(D_MODEL), flat_spec(D_ATT), flat_spec(D_ATT),
                   pl.BlockSpec((nb, 8, D_CONV), lambda i: (i, 0, 0))],
        out_shape=[jax.ShapeDtypeStruct((dec_batch * dec_seq, D_MODEL), F32),
                   jax.ShapeDtypeStruct((dec_batch * dec_seq, D_ATT), F32),
                   jax.ShapeDtypeStruct((dec_batch * dec_seq, D_ATT), F32),
                   jax.ShapeDtypeStruct((dec_batch, 8, D_CONV), F32)],
        scratch_shapes=[pltpu.VMEM((nb, GROUP, D_ATT), BF16), pltpu.VMEM((nb, GROUP, D_ATT), BF16),
                        pltpu.VMEM((N_SLABS, SLAB_HEADS * dec_seq, KWIN), F32)],
        compiler_params=pltpu.CompilerParams(dimension_semantics=("arbitrary",),
                                             vmem_limit_bytes=VMEM_LIMIT_BYTES),
        name="sample_layer",
    )(x_sample.reshape(dec_batch * dec_seq, D_MODEL), p_sample[0].reshape(dec_batch * dec_seq, D_PLE),
      jnp.transpose(cache_k[0], (0, 2, 3, 1)), jnp.transpose(cache_v[0], (0, 2, 3, 1)),
      state_conv[0], *weights, gvec)

    return (y_p,
            y_s.reshape(dec_batch, dec_seq, D_MODEL),
            k_p.reshape(1, batch, HIST, N_HEADS, HEAD_DIM),
            v_p.reshape(1, batch, HIST, N_HEADS, HEAD_DIM),
            c_p[None, :, 8 - (CONV_W - 1):, :],
            k_s.reshape(1, dec_batch, dec_seq, N_HEADS, HEAD_DIM),
            v_s.reshape(1, dec_batch, dec_seq, N_HEADS, HEAD_DIM),
            c_s[None, :, 8 - (CONV_W - 1):, :])
```

```python
import jax
import jax.numpy as jnp
from jax import lax
from jax.experimental import pallas as pl
from jax.experimental.pallas import tpu as pltpu

D_MODEL = 1024
D_CONV = 512
D_ATT = 512
HEAD_DIM = 64
HEAD_SHIFT = 6
N_HEADS = 8
SLAB_HEADS = 4
N_SLABS = N_HEADS // SLAB_HEADS
LANES = 128
SLAB_W = SLAB_HEADS * HEAD_DIM
CHUNK = 64
N_PAST_CHUNKS = 8
HIST = N_PAST_CHUNKS * CHUNK
REL_CLIP = 128
D_PLE = 256
CONV_W = 3
EPS = 1e-6
NEG = -1e30
SCALE = HEAD_DIM ** -0.5
SECTION_W = 512
SEC_H, SEC_B, SEC_C, SEC_ZC, SEC_Q, SEC_K, SEC_V, SEC_ZA = range(8)

GROUP = 2 * CHUNK
KWIN = HIST + GROUP
KBLOCKS = KWIN // LANES
FLAT = KWIN - REL_CLIP
PROMPT_TILE = 512
HALF = 512
SAMPLE_BATCH_PER_STEP = 4
VMEM_LIMIT_BYTES = 56 * 1024 * 1024

F32 = jnp.float32
BF16 = jnp.bfloat16


def _rms(x, g):
    ms = jnp.mean(x * x, axis=-1, keepdims=True)
    return x * lax.rsqrt(ms + EPS) * g


def _sigmoid(x):
    return 1.0 / (1.0 + jnp.exp(-x))


def _dot(a, b):
    return jnp.dot(a, b, preferred_element_type=F32)


def _run(steps):
    try:
        while True:
            next(steps)
    except StopIteration as done:
        return done.value


def _chain(*stages):
    results = []
    for steps in stages:
        results.append((yield from steps))
    return results


def _zip(main, side, side_per_main):
    side_val, side_done = None, False
    while True:
        try:
            next(main)
        except StopIteration as done:
            main_val = done.value
            break
        for _ in range(side_per_main):
            if not side_done:
                try:
                    next(side)
                except StopIteration as done:
                    side_val, side_done = done.value, True
    if not side_done:
        side_val = _run(side)
    return main_val, side_val


def _proj(xn, win_ref, section):
    return _dot(xn, win_ref[:, SECTION_W * section:SECTION_W * (section + 1)])


def _qkv(x, win_ref, nin_ref):
    xn = _rms(x, nin_ref[...]).astype(BF16)
    q = (_proj(xn, win_ref, SEC_Q) * SCALE).astype(BF16)
    yield
    k = _proj(xn, win_ref, SEC_K)
    yield
    v = _proj(xn, win_ref, SEC_V)
    return xn, q, k, v


def _conv_gate(xn, win_ref, cw_ref, ncv_ref, prev_rows):
    h = _proj(xn, win_ref, SEC_H)
    yield
    u = _proj(xn, win_ref, SEC_C) * h
    u1, u2 = prev_rows(u)
    cv = cw_ref[0:1, :] * u2 + cw_ref[1:2, :] * u1 + cw_ref[2:3, :] * u
    yield
    zc = _proj(xn, win_ref, SEC_ZC)
    gz = zc * _sigmoid(zc)
    yield
    yc = _proj(xn, win_ref, SEC_B) * cv * gz
    yc_n = _rms(yc, ncv_ref[...])
    yield
    za = _proj(xn, win_ref, SEC_ZA)
    return u, yc_n, za


def _dot_t(a, b):
    return lax.dot_general(a, b, (((1,), (1,)), ((), ())), preferred_element_type=F32)


def _attend(members):
    r = members[0][0].shape[0]
    lane = lax.broadcasted_iota(jnp.int32, members[0][0].shape, 1) >> HEAD_SHIFT
    scores = []
    for qslab, keys, _, bias in members:
        zero = jnp.zeros_like(qslab)
        qs = jnp.concatenate([jnp.where(lane == j, qslab, zero) for j in range(SLAB_HEADS)], axis=0)
        parts = [_dot(qs, k) if transposed else _dot_t(qs, k) for k, transposed in keys]
        s = parts[0] if len(parts) == 1 else jnp.concatenate(parts, axis=1)
        scores.append(s + bias)
    s = scores[0] if len(scores) == 1 else jnp.concatenate(scores, axis=0)
    m = jnp.max(s, axis=-1, keepdims=True)
    e = jnp.exp(s - m)
    inv_l = 1.0 / jnp.sum(e, axis=-1, keepdims=True)
    p = e.astype(BF16)
    outs = []
    for i, (_, _, values, _) in enumerate(members):
        rows = slice(i * SLAB_HEADS * r, (i + 1) * SLAB_HEADS * r)
        o, col = None, 0
        for v, transposed in values:
            n = v.shape[1] if transposed else v.shape[0]
            pv = _dot_t(p[rows, col:col + n], v) if transposed else _dot(p[rows, col:col + n], v)
            o = pv if o is None else o + pv
            col += n
        o = o * inv_l[rows]
        out = o[(SLAB_HEADS - 1) * r:]
        for j in reversed(range(SLAB_HEADS - 1)):
            out = jnp.where(lane <= j, o[j * r:(j + 1) * r], out)
        outs.append(out)
    return outs


def _build_bias(gvec_ref, n_rows, in_band, store):
    var0 = FLAT - GROUP
    width = KWIN - var0 + GROUP
    i = lax.broadcasted_iota(jnp.int32, (n_rows, width), 0)
    band = in_band(lax.broadcasted_iota(jnp.int32, (n_rows, KWIN), 0),
                   lax.broadcasted_iota(jnp.int32, (n_rows, KWIN), 1))
    for h in range(N_HEADS):
        x = jnp.broadcast_to(gvec_ref[h:h + 1, var0:], (n_rows, width))
        bit = 0
        while (1 << bit) < n_rows:
            x = jnp.where(((i >> bit) & 1) == 1, pltpu.roll(x, 1 << bit, 1), x)
            bit += 1
        flat = jnp.broadcast_to(gvec_ref[h:h + 1, 0:var0], (n_rows, var0))
        store(h, jnp.where(band, jnp.concatenate([flat, x[:, GROUP:]], axis=1), NEG))


def _tail(x, yc_n, ya, za, p, wout_ref, wg_ref, wp_ref, nat_ref, npl_ref, nfin_ref):
    pe = _dot(p.astype(BF16), wp_ref[...])
    ya_n = _rms(ya * (za * _sigmoid(za)), nat_ref[...])
    yield
    y = _dot(yc_n.astype(BF16), wout_ref[0:D_CONV, :]) + _dot(ya_n.astype(BF16), wout_ref[D_CONV:, :])
    x1 = x + y
    yield
    gate = _sigmoid(_dot(_rms(x1, npl_ref[...]).astype(BF16), wg_ref[...]))
    x2 = x1 + gate * pe
    return _rms(x2, nfin_ref[...])


def _prompt_kernel(x_ref, p_ref, win_ref, wout_ref, wg_ref, wp_ref, nin_ref, cw_ref, ncv_ref,
                   nat_ref, npl_ref, nfin_ref, gvec_ref,
                   y_ref, ks_ref, vs_ref, cs_ref, kbuf, vbuf, ucar, unext, bias_ref):
    t = pl.program_id(1)
    tile = x_ref.shape[0]

    @pl.when((pl.program_id(0) == 0) & (t == 0))
    def _():
        def store(h, tb):
            j = h % SLAB_HEADS
            for cb in range(KBLOCKS):
                bias_ref[h // SLAB_HEADS, cb, j * GROUP:(j + 1) * GROUP, :] = tb[:, cb * LANES:(cb + 1) * LANES]

        _build_bias(gvec_ref, GROUP,
                    lambda i, s: ((i < CHUNK) & (s < HIST + CHUNK)) | ((i >= CHUNK) & (s >= CHUNK)), store)

    @pl.when(t > 0)
    def _():
        kbuf[0:HIST, :] = kbuf[tile:, :]
        vbuf[0:HIST, :] = vbuf[tile:, :]
        ucar[...] = unext[...]

    n_half = tile // HALF
    row = lax.broadcasted_iota(jnp.int32, (HALF, D_CONV), 0)

    def qkv(h):
        rows = slice(h * HALF, (h + 1) * HALF)
        xn, q, k, v = yield from _qkv(x_ref[rows, :], win_ref, nin_ref)
        kbuf[HIST + h * HALF:HIST + (h + 1) * HALF, :] = k.astype(BF16)
        vbuf[HIST + h * HALF:HIST + (h + 1) * HALF, :] = v.astype(BF16)
        ks_ref[rows, :] = k
        vs_ref[rows, :] = v
        return xn, q

    def conv_gate(xn, c1, c2):
        def prev_rows(u):
            u1 = jnp.where(row == 0, c1, pltpu.roll(u, 1, 0))
            u2 = jnp.where(row == 0, c2, jnp.where(row == 1, c1, pltpu.roll(u, 2, 0)))
            return u1, u2

        return _conv_gate(xn, win_ref, cw_ref, ncv_ref, prev_rows)

    def attention(h, q, first_tile):
        groups = []
        for gl in range(HALF // GROUP):
            g = h * (HALF // GROUP) + gl
            cb0 = max(0, (HIST - g * GROUP) // LANES) if first_tile else 0
            key_rows = slice(g * GROUP + cb0 * LANES, g * GROUP + KWIN)
            slabs = []
            for p in range(N_SLABS):
                lanes = slice(p * SLAB_W, (p + 1) * SLAB_W)
                bias = jnp.concatenate([bias_ref[p, cb] for cb in range(cb0, KBLOCKS)], axis=1)
                slabs.extend(_attend([(q[gl * GROUP:(gl + 1) * GROUP, lanes],
                                       [(kbuf[key_rows, lanes], False)], [(vbuf[key_rows, lanes], False)], bias)]))
                yield
            groups.append(jnp.concatenate(slabs, axis=1))
        return jnp.concatenate(groups, axis=0)

    def tail(h, yc_n, ya, za):
        rows = slice(h * HALF, (h + 1) * HALF)
        y_ref[rows, :] = yield from _tail(x_ref[rows, :], yc_n, ya, za, p_ref[rows, :],
                                          wout_ref, wg_ref, wp_ref, nat_ref, npl_ref, nfin_ref)

    def layer(first_tile):
        xn, q = _run(qkv(0))
        if first_tile:
            c1 = c2 = jnp.zeros((1, D_CONV), F32)
        else:
            c1, c2 = ucar[7:8, :], ucar[6:7, :]
        done = None
        for h in range(n_half):
            side = [conv_gate(xn, c1, c2)]
            if h + 1 < n_half:
                side.append(qkv(h + 1))
            if done is not None:
                side.append(tail(h - 1, *done))
            ya, results = _zip(attention(h, q, first_tile), _chain(*side), 2)
            u, yc_n, za = results[0]
            done = (yc_n, ya, za)
            c1, c2 = u[HALF - 1:HALF, :], u[HALF - 2:HALF - 1, :]
            if h + 1 < n_half:
                xn, q = results[1]
        _run(tail(n_half - 1, *done))
        unext[...] = u[HALF - 8:, :]
        cs_ref[...] = u[HALF - 8:, :]

    pl.when(t == 0)(lambda: layer(True))
    pl.when(t > 0)(lambda: layer(False))


def _sample_kernel(x_ref, p_ref, ck_ref, cv_ref, sc_ref, win_ref, wout_ref, wg_ref, wp_ref, nin_ref,
                   cw_ref, ncv_ref, nat_ref, npl_ref, nfin_ref, gvec_ref,
                   y_ref, ks_ref, vs_ref, cs_ref, knew, vnew, bias_ref):
    nb = ck_ref.shape[0]
    rows = x_ref.shape[0]
    seq = rows // nb

    @pl.when(pl.program_id(0) == 0)
    def _():
        def store(h, tb):
            j = h % SLAB_HEADS
            bias_ref[h // SLAB_HEADS, j * seq:(j + 1) * seq, :] = tb

        _build_bias(gvec_ref, seq, lambda i, s: s < HIST + seq, store)
        knew[:, seq:, :] = jnp.zeros((nb, GROUP - seq, D_ATT), BF16)
        vnew[:, seq:, :] = jnp.zeros((nb, GROUP - seq, D_ATT), BF16)

    x = x_ref[...]
    rmod = lax.broadcasted_iota(jnp.int32, (rows, D_CONV), 0) % seq

    def prev_rows(u):
        e1 = jnp.concatenate([jnp.broadcast_to(sc_ref[j, 1:2, :], (seq, D_CONV)) for j in range(nb)], axis=0)
        e2 = jnp.concatenate([jnp.broadcast_to(sc_ref[j, 0:1, :], (seq, D_CONV)) for j in range(nb)], axis=0)
        u1 = jnp.where(rmod == 0, e1, pltpu.roll(u, 1, 0))
        u2 = jnp.where(rmod == 0, e2, jnp.where(rmod == 1, e1, pltpu.roll(u, 2, 0)))
        return u1, u2

    xn, q, k, v = _run(_qkv(x, win_ref, nin_ref))
    ks_ref[...] = k
    vs_ref[...] = v
    kb = k.astype(BF16)
    vb = v.astype(BF16)

    def attention():
        cache_kt, cache_vt = [], []
        for j in range(nb):
            rs = slice(j * seq, (j + 1) * seq)
            knew[j, 0:seq, :] = kb[rs, :]
            vnew[j, 0:seq, :] = vb[rs, :]
            cache_kt.append(ck_ref[j].reshape(D_ATT, HIST).astype(BF16))
            cache_vt.append(cv_ref[j].reshape(D_ATT, HIST).astype(BF16))
        slabs = []
        for p in range(N_SLABS):
            lanes = slice(p * SLAB_W, (p + 1) * SLAB_W)
            outs = _attend([(q[j * seq:(j + 1) * seq, lanes],
                             [(cache_kt[j][lanes, :], True), (knew[j, :, lanes], False)],
                             [(cache_vt[j][lanes, :], True), (vnew[j, :, lanes], False)], bias_ref[p])
                            for j in range(nb)])
            slabs.append(jnp.concatenate(outs, axis=0))
            yield
        return jnp.concatenate(slabs, axis=1)

    ya, (u, yc_n, za) = _zip(attention(), _conv_gate(xn, win_ref, cw_ref, ncv_ref, prev_rows), 3)
    for j in range(nb):
        cs_ref[j] = u[(j + 1) * seq - 8:(j + 1) * seq, :]
    y_ref[...] = _run(_tail(x, yc_n, ya, za, p_ref[...], wout_ref, wg_ref, wp_ref, nat_ref, npl_ref, nfin_ref))


def _bias_vector(rel_bias):
    width = KWIN + GROUP
    assert width - FLAT == 2 * REL_CLIP
    return jnp.concatenate([jnp.broadcast_to(rel_bias[:, 2 * REL_CLIP:], (N_HEADS, FLAT)),
                            rel_bias[:, ::-1][:, :2 * REL_CLIP]], axis=1)


def _const_spec(shape):
    nd = len(shape)
    return pl.BlockSpec(shape, lambda *_: (0,) * nd, pipeline_mode=pl.Buffered(1))


def kernel(x_prompt, x_sample, cache_k, cache_v, state_conv, p_prompt, p_sample, norm_in, w_in, conv_w,
           rel_bias, norm_conv, norm_att, w_out, ple_norm, w_ple_gate, w_ple_proj, final_norm):
    depth = w_in.shape[0]
    assert depth == 1, "single-layer kernel"
    batch, seq, _ = x_prompt.shape
    dec_batch, dec_seq, _ = x_sample.shape
    kv_win = cache_k.shape[2]
    assert kv_win == HIST and PROMPT_TILE == HIST and seq % PROMPT_TILE == 0
    assert PROMPT_TILE % HALF == 0 and HALF % GROUP == 0 and (1 << HEAD_SHIFT) == HEAD_DIM
    assert SECTION_W == D_CONV == D_ATT and w_in.shape[2] == 8 * SECTION_W
    assert dec_seq % 8 == 0 and dec_seq <= GROUP and dec_batch % SAMPLE_BATCH_PER_STEP == 0

    win = w_in[0].astype(BF16)
    wout = w_out[0].astype(BF16)
    wg = w_ple_gate[0].astype(BF16)
    wp = w_ple_proj[0].astype(BF16)
    nin = norm_in[0][None, :]
    ncv = norm_conv[0][None, :]
    nat = norm_att[0][None, :]
    npl = ple_norm[0][None, :]
    nfin = final_norm[None, :]
    cw = conv_w[0]
    gvec = _bias_vector(rel_bias[0])

    weight_specs = [
        _const_spec(win.shape), _const_spec(wout.shape), _const_spec(wg.shape), _const_spec(wp.shape),
        _const_spec(nin.shape), _const_spec(cw.shape), _const_spec(ncv.shape), _const_spec(nat.shape),
        _const_spec(npl.shape), _const_spec(nfin.shape),
    ]
    weights = (win, wout, wg, wp, nin, cw, ncv, nat, npl, nfin)

    n_t = seq // PROMPT_TILE
    row_spec = lambda w: pl.BlockSpec((None, PROMPT_TILE, w), lambda b, t: (b, t, 0))
    state_spec = pl.BlockSpec((None, HIST, D_ATT), lambda b, t: (b, 0, 0))
    y_p, k_p, v_p, c_p = pl.pallas_call(
        _prompt_kernel,
        grid=(batch, n_t),
        in_specs=[row_spec(D_MODEL), row_spec(D_PLE)] + weight_specs + [_const_spec(gvec.shape)],
        out_specs=[row_spec(D_MODEL), state_spec, state_spec,
                   pl.BlockSpec((None, 8, D_CONV), lambda b, t: (b, 0, 0))],
        out_shape=[jax.ShapeDtypeStruct((batch, seq, D_MODEL), F32),
                   jax.ShapeDtypeStruct((batch, HIST, D_ATT), F32),
                   jax.ShapeDtypeStruct((batch, HIST, D_ATT), F32),
                   jax.ShapeDtypeStruct((batch, 8, D_CONV), F32)],
        scratch_shapes=[pltpu.VMEM((HIST + PROMPT_TILE, D_ATT), BF16),
                        pltpu.VMEM((HIST + PROMPT_TILE, D_ATT), BF16),
                        pltpu.VMEM((8, D_CONV), F32),
                        pltpu.VMEM((8, D_CONV), F32),
                        pltpu.VMEM((N_SLABS, KBLOCKS, SLAB_HEADS * GROUP, LANES), F32)],
        compiler_params=pltpu.CompilerParams(dimension_semantics=("arbitrary", "arbitrary"),
                                             vmem_limit_bytes=VMEM_LIMIT_BYTES),
        name="prompt_layer",
    )(x_prompt, p_prompt[0], *weights, gvec)

    nb = SAMPLE_BATCH_PER_STEP
    rows = nb * dec_seq
    flat_spec = lambda w: pl.BlockSpec((rows, w), lambda i: (i, 0))
    cache_spec = pl.BlockSpec((nb, N_HEADS, HEAD_DIM, HIST), lambda i: (i, 0, 0, 0))
    y_s, k_s, v_s, c_s = pl.pallas_call(
        _sample_kernel,
        grid=(dec_batch // nb,),
        in_specs=[flat_spec(D_MODEL), flat_spec(D_PLE), cache_spec, cache_spec,
                  pl.BlockSpec((nb, CONV_W - 1, D_CONV), lambda i: (i, 0, 0))]
                 + weight_specs + [_const_spec(gvec.shape)],
        out_specs=[flat_spec(D_MODEL), flat_spec(D_ATT), flat_spec(D_ATT),
                   pl.BlockSpec((nb, 8, D_CONV), lambda i: (i, 0, 0))],
        out_shape=[jax.ShapeDtypeStruct((dec_batch * dec_seq, D_MODEL), F32),
                   jax.ShapeDtypeStruct((dec_batch * dec_seq, D_ATT), F32),
                   jax.ShapeDtypeStruct((dec_batch * dec_seq, D_ATT), F32),
                   jax.ShapeDtypeStruct((dec_batch, 8, D_CONV), F32)],
        scratch_shapes=[pltpu.VMEM((nb, GROUP, D_ATT), BF16), pltpu.VMEM((nb, GROUP, D_ATT), BF16),
                        pltpu.VMEM((N_SLABS, SLAB_HEADS * dec_seq, KWIN), F32)],
        compiler_params=pltpu.CompilerParams(dimension_semantics=("arbitrary",),
                                             vmem_limit_bytes=VMEM_LIMIT_BYTES),
        name="sample_layer",
    )(x_sample.reshape(dec_batch * dec_seq, D_MODEL), p_sample[0].reshape(dec_batch * dec_seq, D_PLE),
      jnp.transpose(cache_k[0], (0, 2, 3, 1)), jnp.transpose(cache_v[0], (0, 2, 3, 1)),
      state_conv[0], *weights, gvec)

    return (y_p,
            y_s.reshape(dec_batch, dec_seq, D_MODEL),
            k_p.reshape(1, batch, HIST, N_HEADS, HEAD_DIM),
            v_p.reshape(1, batch, HIST, N_HEADS, HEAD_DIM),
            c_p[None, :, 8 - (CONV_W - 1):, :],
            k_s.reshape(1, dec_batch, dec_seq, N_HEADS, HEAD_DIM),
            v_s.reshape(1, dec_batch, dec_seq, N_HEADS, HEAD_DIM),
            c_s[None, :, 8 - (CONV_W - 1):, :])
```

```python
import jax
import jax.numpy as jnp
from jax import lax
from jax.experimental import pallas as pl
from jax.experimental.pallas import tpu as pltpu

D_MODEL = 1024
D_CONV = 512
D_ATT = 512
HEAD_DIM = 64
HEAD_SHIFT = 6
N_HEADS = 8
SLAB_HEADS = 4
N_SLABS = N_HEADS // SLAB_HEADS
LANES = 128
SLAB_W = SLAB_HEADS * HEAD_DIM
CHUNK = 64
N_PAST_CHUNKS = 8
HIST = N_PAST_CHUNKS * CHUNK
REL_CLIP = 128
D_PLE = 256
CONV_W = 3
EPS = 1e-6
NEG = -1e30
SCALE = HEAD_DIM ** -0.5
SECTION_W = 512
SEC_H, SEC_B, SEC_C, SEC_ZC, SEC_Q, SEC_K, SEC_V, SEC_ZA = range(8)

GROUP = 2 * CHUNK
KWIN = HIST + GROUP
KBLOCKS = KWIN // LANES
FLAT = KWIN - REL_CLIP
PROMPT_TILE = 512
UNIT = 512
SAMPLE_BATCH_PER_STEP = 4
VMEM_LIMIT_BYTES = 56 * 1024 * 1024

F32 = jnp.float32
BF16 = jnp.bfloat16


def _rms(x, g):
    ms = jnp.mean(x * x, axis=-1, keepdims=True)
    return x * lax.rsqrt(ms + EPS) * g


def _sigmoid(x):
    return 1.0 / (1.0 + jnp.exp(-x))


def _dot(a, b):
    return jnp.dot(a, b, preferred_element_type=F32)


def _run(steps):
    try:
        while True:
            next(steps)
    except StopIteration as done:
        return done.value


def _chain(*stages):
    results = []
    for steps in stages:
        results.append((yield from steps))
    return results


def _zip(main, side, side_per_main):
    side_val, side_done = None, False
    while True:
        try:
            next(main)
        except StopIteration as done:
            main_val = done.value
            break
        for _ in range(side_per_main):
            if not side_done:
                try:
                    next(side)
                except StopIteration as done:
                    side_val, side_done = done.value, True
    if not side_done:
        side_val = _run(side)
    return main_val, side_val


def _proj(xn, win_ref, section):
    return _dot(xn, win_ref[:, SECTION_W * section:SECTION_W * (section + 1)])


def _qkv(x, win_ref, nin_ref):
    xn = _rms(x, nin_ref[...]).astype(BF16)
    q = (_proj(xn, win_ref, SEC_Q) * SCALE).astype(BF16)
    yield
    k = _proj(xn, win_ref, SEC_K)
    yield
    v = _proj(xn, win_ref, SEC_V)
    return xn, q, k, v


def _conv_gate(xn, win_ref, cw_ref, ncv_ref, prev_rows):
    h = _proj(xn, win_ref, SEC_H)
    yield
    u = _proj(xn, win_ref, SEC_C) * h
    u1, u2 = prev_rows(u)
    cv = cw_ref[0:1, :] * u2 + cw_ref[1:2, :] * u1 + cw_ref[2:3, :] * u
    yield
    zc = _proj(xn, win_ref, SEC_ZC)
    gz = zc * _sigmoid(zc)
    yield
    yc = _proj(xn, win_ref, SEC_B) * cv * gz
    yc_n = _rms(yc, ncv_ref[...])
    yield
    za = _proj(xn, win_ref, SEC_ZA)
    return u, yc_n, za


def _dot_t(a, b):
    return lax.dot_general(a, b, (((1,), (1,)), ((), ())), preferred_element_type=F32)


def _attend(members):
    r = members[0][0].shape[0]
    lane = lax.broadcasted_iota(jnp.int32, members[0][0].shape, 1) >> HEAD_SHIFT
    scores = []
    for qslab, keys, _, bias in members:
        zero = jnp.zeros_like(qslab)
        qs = jnp.concatenate([jnp.where(lane == j, qslab, zero) for j in range(SLAB_HEADS)], axis=0)
        parts = [_dot(qs, k) if transposed else _dot_t(qs, k) for k, transposed in keys]
        s = parts[0] if len(parts) == 1 else jnp.concatenate(parts, axis=1)
        scores.append(s + bias)
    s = scores[0] if len(scores) == 1 else jnp.concatenate(scores, axis=0)
    m = jnp.max(s, axis=-1, keepdims=True)
    e = jnp.exp(s - m)
    inv_l = 1.0 / jnp.sum(e, axis=-1, keepdims=True)
    p = e.astype(BF16)
    outs = []
    for i, (_, _, values, _) in enumerate(members):
        rows = slice(i * SLAB_HEADS * r, (i + 1) * SLAB_HEADS * r)
        o, col = None, 0
        for v, transposed in values:
            n = v.shape[1] if transposed else v.shape[0]
            pv = _dot_t(p[rows, col:col + n], v) if transposed else _dot(p[rows, col:col + n], v)
            o = pv if o is None else o + pv
            col += n
        o = o * inv_l[rows]
        out = o[(SLAB_HEADS - 1) * r:]
        for j in reversed(range(SLAB_HEADS - 1)):
            out = jnp.where(lane <= j, o[j * r:(j + 1) * r], out)
        outs.append(out)
    return outs


def _build_bias(gvec_ref, n_rows, in_band, store):
    var0 = FLAT - GROUP
    width = KWIN - var0 + GROUP
    i = lax.broadcasted_iota(jnp.int32, (n_rows, width), 0)
    band = in_band(lax.broadcasted_iota(jnp.int32, (n_rows, KWIN), 0),
                   lax.broadcasted_iota(jnp.int32, (n_rows, KWIN), 1))
    for h in range(N_HEADS):
        x = jnp.broadcast_to(gvec_ref[h:h + 1, var0:], (n_rows, width))
        bit = 0
        while (1 << bit) < n_rows:
            x = jnp.where(((i >> bit) & 1) == 1, pltpu.roll(x, 1 << bit, 1), x)
            bit += 1
        flat = jnp.broadcast_to(gvec_ref[h:h + 1, 0:var0], (n_rows, var0))
        store(h, jnp.where(band, jnp.concatenate([flat, x[:, GROUP:]], axis=1), NEG))


def _tail(x, yc_n, ya, za, p, wout_ref, wg_ref, wp_ref, nat_ref, npl_ref, nfin_ref):
    pe = _dot(p.astype(BF16), wp_ref[...])
    ya_n = _rms(ya * (za * _sigmoid(za)), nat_ref[...])
    yield
    y = _dot(yc_n.astype(BF16), wout_ref[0:D_CONV, :]) + _dot(ya_n.astype(BF16), wout_ref[D_CONV:, :])
    x1 = x + y
    yield
    gate = _sigmoid(_dot(_rms(x1, npl_ref[...]).astype(BF16), wg_ref[...]))
    x2 = x1 + gate * pe
    return _rms(x2, nfin_ref[...])


def _prompt_kernel(x_ref, p_ref, win_ref, wout_ref, wg_ref, wp_ref, nin_ref, cw_ref, ncv_ref,
                   nat_ref, npl_ref, nfin_ref, gvec_ref,
                   y_ref, ks_ref, vs_ref, cs_ref, kbuf, vbuf, ucar, unext, bias_ref):
    t = pl.program_id(1)
    tile = x_ref.shape[0]

    @pl.when((pl.program_id(0) == 0) & (t == 0))
    def _():
        def store(h, tb):
            j = h % SLAB_HEADS
            for cb in range(KBLOCKS):
                bias_ref[h // SLAB_HEADS, cb, j * GROUP:(j + 1) * GROUP, :] = tb[:, cb * LANES:(cb + 1) * LANES]

        _build_bias(gvec_ref, GROUP,
                    lambda i, s: ((i < CHUNK) & (s < HIST + CHUNK)) | ((i >= CHUNK) & (s >= CHUNK)), store)

    @pl.when(t > 0)
    def _():
        kbuf[0:HIST, :] = kbuf[tile:, :]
        vbuf[0:HIST, :] = vbuf[tile:, :]
        ucar[...] = unext[...]

    n_units = tile // UNIT
    row = lax.broadcasted_iota(jnp.int32, (UNIT, D_CONV), 0)

    def qkv(h):
        rows = slice(h * UNIT, (h + 1) * UNIT)
        xn, q, k, v = yield from _qkv(x_ref[rows, :], win_ref, nin_ref)
        kbuf[HIST + h * UNIT:HIST + (h + 1) * UNIT, :] = k.astype(BF16)
        vbuf[HIST + h * UNIT:HIST + (h + 1) * UNIT, :] = v.astype(BF16)
        ks_ref[rows, :] = k
        vs_ref[rows, :] = v
        return xn, q

    def conv_gate(xn, c1, c2):
        def prev_rows(u):
            u1 = jnp.where(row == 0, c1, pltpu.roll(u, 1, 0))
            u2 = jnp.where(row == 0, c2, jnp.where(row == 1, c1, pltpu.roll(u, 2, 0)))
            return u1, u2

        return _conv_gate(xn, win_ref, cw_ref, ncv_ref, prev_rows)

    def attention(h, q, first_tile):
        groups = []
        for gl in range(UNIT // GROUP):
            g = h * (UNIT // GROUP) + gl
            cb0 = max(0, (HIST - g * GROUP) // LANES) if first_tile else 0
            key_rows = slice(g * GROUP + cb0 * LANES, g * GROUP + KWIN)
            slabs = []
            for p in range(N_SLABS):
                lanes = slice(p * SLAB_W, (p + 1) * SLAB_W)
                bias = jnp.concatenate([bias_ref[p, cb] for cb in range(cb0, KBLOCKS)], axis=1)
                slabs.extend(_attend([(q[gl * GROUP:(gl + 1) * GROUP, lanes],
                                       [(kbuf[key_rows, lanes], False)], [(vbuf[key_rows, lanes], False)], bias)]))
                yield
            groups.append(jnp.concatenate(slabs, axis=1))
        return jnp.concatenate(groups, axis=0)

    def tail(h, yc_n, ya, za):
        rows = slice(h * UNIT, (h + 1) * UNIT)
        y_ref[rows, :] = yield from _tail(x_ref[rows, :], yc_n, ya, za, p_ref[rows, :],
                                          wout_ref, wg_ref, wp_ref, nat_ref, npl_ref, nfin_ref)

    def layer(first_tile):
        xn, q = _run(qkv(0))
        if first_tile:
            c1 = c2 = jnp.zeros((1, D_CONV), F32)
        else:
            c1, c2 = ucar[7:8, :], ucar[6:7, :]
        done = None
        for h in range(n_units):
            side = [conv_gate(xn, c1, c2)]
            if h + 1 < n_units:
                side.append(qkv(h + 1))
            if done is not None:
                side.append(tail(h - 1, *done))
            ya, results = _zip(attention(h, q, first_tile), _chain(*side), 5)
            u, yc_n, za = results[0]
            done = (yc_n, ya, za)
            c1, c2 = u[UNIT - 1:UNIT, :], u[UNIT - 2:UNIT - 1, :]
            if h + 1 < n_units:
                xn, q = results[1]
        _run(tail(n_units - 1, *done))
        unext[...] = u[UNIT - 8:, :]
        cs_ref[...] = u[UNIT - 8:, :]

    pl.when(t == 0)(lambda: layer(True))
    pl.when(t > 0)(lambda: layer(False))


def _sample_kernel(x_ref, p_ref, ck_ref, cv_ref, sc_ref, win_ref, wout_ref, wg_ref, wp_ref, nin_ref,
                   cw_ref, ncv_ref, nat_ref, npl_ref, nfin_ref, gvec_ref,
                   y_ref, ks_ref, vs_ref, cs_ref, knew, vnew, bias_ref):
    nb = ck_ref.shape[0]
    rows = x_ref.shape[0]
    seq = rows // nb

    @pl.when(pl.program_id(0) == 0)
    def _():
        def store(h, tb):
            j = h % SLAB_HEADS
            bias_ref[h // SLAB_HEADS, j * seq:(j + 1) * seq, :] = tb

        _build_bias(gvec_ref, seq, lambda i, s: s < HIST + seq, store)
        knew[:, seq:, :] = jnp.zeros((nb, GROUP - seq, D_ATT), BF16)
        vnew[:, seq:, :] = jnp.zeros((nb, GROUP - seq, D_ATT), BF16)

    x = x_ref[...]
    rmod = lax.broadcasted_iota(jnp.int32, (rows, D_CONV), 0) % seq

    def prev_rows(u):
        e1 = jnp.concatenate([jnp.broadcast_to(sc_ref[j, 1:2, :], (seq, D_CONV)) for j in range(nb)], axis=0)
        e2 = jnp.concatenate([jnp.broadcast_to(sc_ref[j, 0:1, :], (seq, D_CONV)) for j in range(nb)], axis=0)
        u1 = jnp.where(rmod == 0, e1, pltpu.roll(u, 1, 0))
        u2 = jnp.where(rmod == 0, e2, jnp.where(rmod == 1, e1, pltpu.roll(u, 2, 0)))
        return u1, u2

    xn, q, k, v = _run(_qkv(x, win_ref, nin_ref))
    ks_ref[...] = k
    vs_ref[...] = v
    kb = k.astype(BF16)
    vb = v.astype(BF16)

    def attention():
        cache_kt, cache_vt = [], []
        for j in range(nb):
            rs = slice(j * seq, (j + 1) * seq)
            knew[j, 0:seq, :] = kb[rs, :]
            vnew[j, 0:seq, :] = vb[rs, :]
            cache_kt.append(ck_ref[j].reshape(D_ATT, HIST).astype(BF16))
            cache_vt.append(cv_ref[j].reshape(D_ATT, HIST).astype(BF16))
        slabs = []
        for p in range(N_SLABS):
            lanes = slice(p * SLAB_W, (p + 1) * SLAB_W)
            outs = _attend([(q[j * seq:(j + 1) * seq, lanes],
                             [(cache_kt[j][lanes, :], True), (knew[j, :, lanes], False)],
                             [(cache_vt[j][lanes, :], True), (vnew[j, :, lanes], False)], bias_ref[p])
                            for j in range(nb)])
            slabs.append(jnp.concatenate(outs, axis=0))
            yield
        return jnp.concatenate(slabs, axis=1)

    ya, (u, yc_n, za) = _zip(attention(), _conv_gate(xn, win_ref, cw_ref, ncv_ref, prev_rows), 3)
    for j in range(nb):
        cs_ref[j] = u[(j + 1) * seq - 8:(j + 1) * seq, :]
    y_ref[...] = _run(_tail(x, yc_n, ya, za, p_ref[...], wout_ref, wg_ref, wp_ref, nat_ref, npl_ref, nfin_ref))


def _bias_vector(rel_bias):
    width = KWIN + GROUP
    assert width - FLAT == 2 * REL_CLIP
    return jnp.concatenate([jnp.broadcast_to(rel_bias[:, 2 * REL_CLIP:], (N_HEADS, FLAT)),
                            rel_bias[:, ::-1][:, :2 * REL_CLIP]], axis=1)


def _const_spec(shape):
    nd = len(shape)
    return pl.BlockSpec(shape, lambda *_: (0,) * nd, pipeline_mode=pl.Buffered(1))


def kernel(x_prompt, x_sample, cache_k, cache_v, state_conv, p_prompt, p_sample, norm_in, w_in, conv_w,
           rel_bias, norm_conv, norm_att, w_out, ple_norm, w_ple_gate, w_ple_proj, final_norm):
    depth = w_in.shape[0]
    assert depth == 1, "single-layer kernel"
    batch, seq, _ = x_prompt.shape
    dec_batch, dec_seq, _ = x_sample.shape
    kv_win = cache_k.shape[2]
    assert kv_win == HIST and PROMPT_TILE == HIST and seq % PROMPT_TILE == 0
    assert PROMPT_TILE % UNIT == 0 and UNIT % GROUP == 0 and (1 << HEAD_SHIFT) == HEAD_DIM
    assert SECTION_W == D_CONV == D_ATT and w_in.shape[2] == 8 * SECTION_W
    assert dec_seq % 8 == 0 and dec_seq <= GROUP and dec_batch % SAMPLE_BATCH_PER_STEP == 0

    win = w_in[0].astype(BF16)
    wout = w_out[0].astype(BF16)
    wg = w_ple_gate[0].astype(BF16)
    wp = w_ple_proj[0].astype(BF16)
    nin = norm_in[0][None, :]
    ncv = norm_conv[0][None, :]
    nat = norm_att[0][None, :]
    npl = ple_norm[0][None, :]
    nfin = final_norm[None, :]
    cw = conv_w[0]
    gvec = _bias_vector(rel_bias[0])

    weight_specs = [
        _const_spec(win.shape), _const_spec(wout.shape), _const_spec(wg.shape), _const_spec(wp.shape),
        _const_spec(nin.shape), _const_spec(cw.shape), _const_spec(ncv.shape), _const_spec(nat.shape),
        _const_spec(npl.shape), _const_spec(nfin.shape),
    ]
    weights = (win, wout, wg, wp, nin, cw, ncv, nat, npl, nfin)

    n_t = seq // PROMPT_TILE
    row_spec = lambda w: pl.BlockSpec((None, PROMPT_TILE, w), lambda b, t: (b, t, 0))
    state_spec = pl.BlockSpec((None, HIST, D_ATT), lambda b, t: (b, 0, 0))
    y_p, k_p, v_p, c_p = pl.pallas_call(
        _prompt_kernel,
        grid=(batch, n_t),
        in_specs=[row_spec(D_MODEL), row_spec(D_PLE)] + weight_specs + [_const_spec(gvec.shape)],
        out_specs=[row_spec(D_MODEL), state_spec, state_spec,
                   pl.BlockSpec((None, 8, D_CONV), lambda b, t: (b, 0, 0))],
        out_shape=[jax.ShapeDtypeStruct((batch, seq, D_MODEL), F32),
                   jax.ShapeDtypeStruct((batch, HIST, D_ATT), F32),
                   jax.ShapeDtypeStruct((batch, HIST, D_ATT), F32),
                   jax.ShapeDtypeStruct((batch, 8, D_CONV), F32)],
        scratch_shapes=[pltpu.VMEM((HIST + PROMPT_TILE, D_ATT), BF16),
                        pltpu.VMEM((HIST + PROMPT_TILE, D_ATT), BF16),
                        pltpu.VMEM((8, D_CONV), F32),
                        pltpu.VMEM((8, D_CONV), F32),
                        pltpu.VMEM((N_SLABS, KBLOCKS, SLAB_HEADS * GROUP, LANES), F32)],
        compiler_params=pltpu.CompilerParams(dimension_semantics=("arbitrary", "arbitrary"),
                                             vmem_limit_bytes=VMEM_LIMIT_BYTES),
        name="prompt_layer",
    )(x_prompt, p_prompt[0], *weights, gvec)

    nb = SAMPLE_BATCH_PER_STEP
    rows = nb * dec_seq
    flat_spec = lambda w: pl.BlockSpec((rows, w), lambda i: (i, 0))
    cache_spec = pl.BlockSpec((nb, N_HEADS, HEAD_DIM, HIST), lambda i: (i, 0, 0, 0))
    y_s, k_s, v_s, c_s = pl.pallas_call(
        _sample_kernel,
        grid=(dec_batch // nb,),
        in_specs=[flat_spec(D_MODEL), flat_spec(D_PLE), cache_spec, cache_spec,
                  pl.BlockSpec((nb, CONV_W - 1, D_CONV), lambda i: (i, 0, 0))]
                 + weight_specs + [_const_spec(gvec.shape)],
        out_specs=[flat_spec(D_MODEL), flat_spec(D_ATT), flat_spec(D_ATT),
                   pl.BlockSpec((nb, 8, D_CONV), lambda i: (i, 0, 0))],
        out_shape=[jax.ShapeDtypeStruct((dec_batch * dec_seq, D_MODEL), F32),
                   jax.ShapeDtypeStruct((dec_batch * dec_seq, D_ATT), F32),
                   jax.ShapeDtypeStruct((dec_batch * dec_seq, D_ATT), F32),
                   jax.ShapeDtypeStruct((dec_batch, 8, D_CONV), F32)],
        scratch_shapes=[pltpu.VMEM((nb, GROUP, D_ATT), BF16), pltpu.VMEM((nb, GROUP, D_ATT), BF16),
                        pltpu.VMEM((N_SLABS, SLAB_HEADS * dec_seq, KWIN), F32)],
        compiler_params=pltpu.CompilerParams(dimension_semantics=("arbitrary",),
                                             vmem_limit_bytes=VMEM_LIMIT_BYTES),
        name="sample_layer",
    )(x_sample.reshape(dec_batch * dec_seq, D_MODEL), p_sample[0].reshape(dec_batch * dec_seq, D_PLE),
      jnp.transpose(cache_k[0], (0, 2, 3, 1)), jnp.transpose(cache_v[0], (0, 2, 3, 1)),
      state_conv[0], *weights, gvec)

    return (y_p,
            y_s.reshape(dec_batch, dec_seq, D_MODEL),
            k_p.reshape(1, batch, HIST, N_HEADS, HEAD_DIM),
            v_p.reshape(1, batch, HIST, N_HEADS, HEAD_DIM),
            c_p[None, :, 8 - (CONV_W - 1):, :],
            k_s.reshape(1, dec_batch, dec_seq, N_HEADS, HEAD_DIM),
            v_s.reshape(1, dec_batch, dec_seq, N_HEADS, HEAD_DIM),
            c_s[None, :, 8 - (CONV_W - 1):, :])
```

```python
import jax
import jax.numpy as jnp
from jax import lax
from jax.experimental import pallas as pl
from jax.experimental.pallas import tpu as pltpu

D_MODEL = 1024
D_CONV = 512
D_ATT = 512
HEAD_DIM = 64
HEAD_SHIFT = 6
N_HEADS = 8
SLAB_HEADS = 4
N_SLABS = N_HEADS // SLAB_HEADS
LANES = 128
SLAB_W = SLAB_HEADS * HEAD_DIM
CHUNK = 64
N_PAST_CHUNKS = 8
HIST = N_PAST_CHUNKS * CHUNK
REL_CLIP = 128
D_PLE = 256
CONV_W = 3
EPS = 1e-6
NEG = -1e30
SCALE = HEAD_DIM ** -0.5
SECTION_W = 512
SEC_H, SEC_B, SEC_C, SEC_ZC, SEC_Q, SEC_K, SEC_V, SEC_ZA = range(8)

GROUP = 2 * CHUNK
KWIN = HIST + GROUP
KBLOCKS = KWIN // LANES
FLAT = KWIN - REL_CLIP
PROMPT_TILE = 512
UNIT = 512
SAMPLE_BATCH_PER_STEP = 4
VMEM_LIMIT_BYTES = 56 * 1024 * 1024

F32 = jnp.float32
BF16 = jnp.bfloat16


def _rms(x, g):
    ms = jnp.mean(x * x, axis=-1, keepdims=True)
    return x * lax.rsqrt(ms + EPS) * g


def _sigmoid(x):
    return 1.0 / (1.0 + jnp.exp(-x))


def _dot(a, b):
    return jnp.dot(a, b, preferred_element_type=F32)


def _run(steps):
    try:
        while True:
            next(steps)
    except StopIteration as done:
        return done.value


def _chain(*stages):
    results = []
    for steps in stages:
        results.append((yield from steps))
    return results


def _zip(main, side, side_per_main):
    side_val, side_done = None, False
    while True:
        try:
            next(main)
        except StopIteration as done:
            main_val = done.value
            break
        for _ in range(side_per_main):
            if not side_done:
                try:
                    next(side)
                except StopIteration as done:
                    side_val, side_done = done.value, True
    if not side_done:
        side_val = _run(side)
    return main_val, side_val


def _proj(xn, win_ref, section):
    return _dot(xn, win_ref[:, SECTION_W * section:SECTION_W * (section + 1)])


def _qkv(x, win_ref, nin_ref):
    xn = _rms(x, nin_ref[...]).astype(BF16)
    q = (_proj(xn, win_ref, SEC_Q) * SCALE).astype(BF16)
    yield
    k = _proj(xn, win_ref, SEC_K)
    yield
    v = _proj(xn, win_ref, SEC_V)
    return xn, q, k, v


def _conv_gate(xn, win_ref, cw_ref, ncv_ref, prev_rows):
    h = _proj(xn, win_ref, SEC_H)
    yield
    u = _proj(xn, win_ref, SEC_C) * h
    u1, u2 = prev_rows(u)
    cv = cw_ref[0:1, :] * u2 + cw_ref[1:2, :] * u1 + cw_ref[2:3, :] * u
    yield
    zc = _proj(xn, win_ref, SEC_ZC)
    gz = zc * _sigmoid(zc)
    yield
    yc = _proj(xn, win_ref, SEC_B) * cv * gz
    yc_n = _rms(yc, ncv_ref[...])
    yield
    za = _proj(xn, win_ref, SEC_ZA)
    return u, yc_n, za


def _dot_t(a, b):
    return lax.dot_general(a, b, (((1,), (1,)), ((), ())), preferred_element_type=F32)


def _attend(members):
    r = members[0][0].shape[0]
    lane = lax.broadcasted_iota(jnp.int32, members[0][0].shape, 1) >> HEAD_SHIFT
    scores = []
    for qslab, keys, _, bias in members:
        zero = jnp.zeros_like(qslab)
        qs = jnp.concatenate([jnp.where(lane == j, qslab, zero) for j in range(SLAB_HEADS)], axis=0)
        parts = [_dot(qs, k) if transposed else _dot_t(qs, k) for k, transposed in keys]
        s = parts[0] if len(parts) == 1 else jnp.concatenate(parts, axis=1)
        scores.append(s + bias)
    s = scores[0] if len(scores) == 1 else jnp.concatenate(scores, axis=0)
    m = jnp.max(s, axis=-1, keepdims=True)
    e = jnp.exp(s - m)
    inv_l = 1.0 / jnp.sum(e, axis=-1, keepdims=True)
    p = e.astype(BF16)
    outs = []
    for i, (_, _, values, _) in enumerate(members):
        rows = slice(i * SLAB_HEADS * r, (i + 1) * SLAB_HEADS * r)
        o, col = None, 0
        for v, transposed in values:
            n = v.shape[1] if transposed else v.shape[0]
            pv = _dot_t(p[rows, col:col + n], v) if transposed else _dot(p[rows, col:col + n], v)
            o = pv if o is None else o + pv
            col += n
        o = o * inv_l[rows]
        out = o[(SLAB_HEADS - 1) * r:]
        for j in reversed(range(SLAB_HEADS - 1)):
            out = jnp.where(lane <= j, o[j * r:(j + 1) * r], out)
        outs.append(out)
    return outs


def _build_bias(gvec_ref, n_rows, in_band, store):
    var0 = FLAT - GROUP
    width = KWIN - var0 + GROUP
    i = lax.broadcasted_iota(jnp.int32, (n_rows, width), 0)
    band = in_band(lax.broadcasted_iota(jnp.int32, (n_rows, KWIN), 0),
                   lax.broadcasted_iota(jnp.int32, (n_rows, KWIN), 1))
    for h in range(N_HEADS):
        x = jnp.broadcast_to(gvec_ref[h:h + 1, var0:], (n_rows, width))
        bit = 0
        while (1 << bit) < n_rows:
            x = jnp.where(((i >> bit) & 1) == 1, pltpu.roll(x, 1 << bit, 1), x)
            bit += 1
        flat = jnp.broadcast_to(gvec_ref[h:h + 1, 0:var0], (n_rows, var0))
        store(h, jnp.where(band, jnp.concatenate([flat, x[:, GROUP:]], axis=1), NEG))


def _tail(x, yc_n, ya, za, p, wout_ref, wg_ref, wp_ref, nat_ref, npl_ref, nfin_ref):
    pe = _dot(p.astype(BF16), wp_ref[...])
    ya_n = _rms(ya * (za * _sigmoid(za)), nat_ref[...])
    yield
    y = _dot(yc_n.astype(BF16), wout_ref[0:D_CONV, :]) + _dot(ya_n.astype(BF16), wout_ref[D_CONV:, :])
    x1 = x + y
    yield
    gate = _sigmoid(_dot(_rms(x1, npl_ref[...]).astype(BF16), wg_ref[...]))
    x2 = x1 + gate * pe
    return _rms(x2, nfin_ref[...])


def _prompt_kernel(x_ref, p_ref, win_ref, wout_ref, wg_ref, wp_ref, nin_ref, cw_ref, ncv_ref,
                   nat_ref, npl_ref, nfin_ref, gvec_ref,
                   y_ref, ks_ref, vs_ref, cs_ref, kbuf, vbuf, ucar, unext, bias_ref):
    t = pl.program_id(1)
    tile = x_ref.shape[0]

    @pl.when((pl.program_id(0) == 0) & (t == 0))
    def _():
        def store(h, tb):
            j = h % SLAB_HEADS
            for cb in range(KBLOCKS):
                bias_ref[h // SLAB_HEADS, cb, j * GROUP:(j + 1) * GROUP, :] = tb[:, cb * LANES:(cb + 1) * LANES]

        _build_bias(gvec_ref, GROUP,
                    lambda i, s: ((i < CHUNK) & (s < HIST + CHUNK)) | ((i >= CHUNK) & (s >= CHUNK)), store)

    @pl.when(t > 0)
    def _():
        kbuf[0:HIST, :] = kbuf[tile:, :]
        vbuf[0:HIST, :] = vbuf[tile:, :]
        ucar[...] = unext[...]

    n_units = tile // UNIT
    row = lax.broadcasted_iota(jnp.int32, (UNIT, D_CONV), 0)

    def qkv(h):
        rows = slice(h * UNIT, (h + 1) * UNIT)
        xn, q, k, v = yield from _qkv(x_ref[rows, :], win_ref, nin_ref)
        kbuf[HIST + h * UNIT:HIST + (h + 1) * UNIT, :] = k.astype(BF16)
        vbuf[HIST + h * UNIT:HIST + (h + 1) * UNIT, :] = v.astype(BF16)
        ks_ref[:, rows] = k.T
        vs_ref[:, rows] = v.T
        return xn, q

    def conv_gate(xn, c1, c2):
        def prev_rows(u):
            u1 = jnp.where(row == 0, c1, pltpu.roll(u, 1, 0))
            u2 = jnp.where(row == 0, c2, jnp.where(row == 1, c1, pltpu.roll(u, 2, 0)))
            return u1, u2

        return _conv_gate(xn, win_ref, cw_ref, ncv_ref, prev_rows)

    def attention(h, q, first_tile):
        groups = []
        for gl in range(UNIT // GROUP):
            g = h * (UNIT // GROUP) + gl
            cb0 = max(0, (HIST - g * GROUP) // LANES) if first_tile else 0
            key_rows = slice(g * GROUP + cb0 * LANES, g * GROUP + KWIN)
            slabs = []
            for p in range(N_SLABS):
                lanes = slice(p * SLAB_W, (p + 1) * SLAB_W)
                bias = jnp.concatenate([bias_ref[p, cb] for cb in range(cb0, KBLOCKS)], axis=1)
                slabs.extend(_attend([(q[gl * GROUP:(gl + 1) * GROUP, lanes],
                                       [(kbuf[key_rows, lanes], False)], [(vbuf[key_rows, lanes], False)], bias)]))
                yield
            groups.append(jnp.concatenate(slabs, axis=1))
        return jnp.concatenate(groups, axis=0)

    def tail(h, yc_n, ya, za):
        rows = slice(h * UNIT, (h + 1) * UNIT)
        y_ref[rows, :] = yield from _tail(x_ref[rows, :], yc_n, ya, za, p_ref[rows, :],
                                          wout_ref, wg_ref, wp_ref, nat_ref, npl_ref, nfin_ref)

    def layer(first_tile):
        xn, q = _run(qkv(0))
        if first_tile:
            c1 = c2 = jnp.zeros((1, D_CONV), F32)
        else:
            c1, c2 = ucar[7:8, :], ucar[6:7, :]
        done = None
        for h in range(n_units):
            side = [conv_gate(xn, c1, c2)]
            if h + 1 < n_units:
                side.append(qkv(h + 1))
            if done is not None:
                side.append(tail(h - 1, *done))
            ya, results = _zip(attention(h, q, first_tile), _chain(*side), 5)
            u, yc_n, za = results[0]
            done = (yc_n, ya, za)
            c1, c2 = u[UNIT - 1:UNIT, :], u[UNIT - 2:UNIT - 1, :]
            if h + 1 < n_units:
                xn, q = results[1]
        _run(tail(n_units - 1, *done))
        unext[...] = u[UNIT - 8:, :]
        cs_ref[...] = u[UNIT - 8:, :]

    pl.when(t == 0)(lambda: layer(True))
    pl.when(t > 0)(lambda: layer(False))


def _sample_kernel(x_ref, p_ref, ck_ref, cv_ref, sc_ref, win_ref, wout_ref, wg_ref, wp_ref, nin_ref,
                   cw_ref, ncv_ref, nat_ref, npl_ref, nfin_ref, gvec_ref,
                   y_ref, ks_ref, vs_ref, cs_ref, knew, vnew, bias_ref):
    nb = ck_ref.shape[0]
    rows = x_ref.shape[0]
    seq = rows // nb

    @pl.when(pl.program_id(0) == 0)
    def _():
        def store(h, tb):
            j = h % SLAB_HEADS
            bias_ref[h // SLAB_HEADS, j * seq:(j + 1) * seq, :] = tb

        _build_bias(gvec_ref, seq, lambda i, s: s < HIST + seq, store)
        knew[:, seq:, :] = jnp.zeros((nb, GROUP - seq, D_ATT), BF16)
        vnew[:, seq:, :] = jnp.zeros((nb, GROUP - seq, D_ATT), BF16)

    x = x_ref[...]
    rmod = lax.broadcasted_iota(jnp.int32, (rows, D_CONV), 0) % seq

    def prev_rows(u):
        e1 = jnp.concatenate([jnp.broadcast_to(sc_ref[j, 1:2, :], (seq, D_CONV)) for j in range(nb)], axis=0)
        e2 = jnp.concatenate([jnp.broadcast_to(sc_ref[j, 0:1, :], (seq, D_CONV)) for j in range(nb)], axis=0)
        u1 = jnp.where(rmod == 0, e1, pltpu.roll(u, 1, 0))
        u2 = jnp.where(rmod == 0, e2, jnp.where(rmod == 1, e1, pltpu.roll(u, 2, 0)))
        return u1, u2

    xn, q, k, v = _run(_qkv(x, win_ref, nin_ref))
    ks_ref[...] = k
    vs_ref[...] = v
    kb = k.astype(BF16)
    vb = v.astype(BF16)

    def attention():
        cache_kt, cache_vt = [], []
        for j in range(nb):
            rs = slice(j * seq, (j + 1) * seq)
            knew[j, 0:seq, :] = kb[rs, :]
            vnew[j, 0:seq, :] = vb[rs, :]
            cache_kt.append(ck_ref[j].reshape(D_ATT, HIST).astype(BF16))
            cache_vt.append(cv_ref[j].reshape(D_ATT, HIST).astype(BF16))
        slabs = []
        for p in range(N_SLABS):
            lanes = slice(p * SLAB_W, (p + 1) * SLAB_W)
            outs = _attend([(q[j * seq:(j + 1) * seq, lanes],
                             [(cache_kt[j][lanes, :], True), (knew[j, :, lanes], False)],
                             [(cache_vt[j][lanes, :], True), (vnew[j, :, lanes], False)], bias_ref[p])
                            for j in range(nb)])
            slabs.append(jnp.concatenate(outs, axis=0))
            yield
        return jnp.concatenate(slabs, axis=1)

    ya, (u, yc_n, za) = _zip(attention(), _conv_gate(xn, win_ref, cw_ref, ncv_ref, prev_rows), 3)
    for j in range(nb):
        cs_ref[j] = u[(j + 1) * seq - 8:(j + 1) * seq, :]
    y_ref[...] = _run(_tail(x, yc_n, ya, za, p_ref[...], wout_ref, wg_ref, wp_ref, nat_ref, npl_ref, nfin_ref))


def _bias_vector(rel_bias):
    width = KWIN + GROUP
    assert width - FLAT == 2 * REL_CLIP
    return jnp.concatenate([jnp.broadcast_to(rel_bias[:, 2 * REL_CLIP:], (N_HEADS, FLAT)),
                            rel_bias[:, ::-1][:, :2 * REL_CLIP]], axis=1)


def _const_spec(shape):
    nd = len(shape)
    return pl.BlockSpec(shape, lambda *_: (0,) * nd, pipeline_mode=pl.Buffered(1))


def kernel(x_prompt, x_sample, cache_k, cache_v, state_conv, p_prompt, p_sample, norm_in, w_in, conv_w,
           rel_bias, norm_conv, norm_att, w_out, ple_norm, w_ple_gate, w_ple_proj, final_norm):
    depth = w_in.shape[0]
    assert depth == 1, "single-layer kernel"
    batch, seq, _ = x_prompt.shape
    dec_batch, dec_seq, _ = x_sample.shape
    kv_win = cache_k.shape[2]
    assert kv_win == HIST and PROMPT_TILE == HIST and seq % PROMPT_TILE == 0
    assert PROMPT_TILE % UNIT == 0 and UNIT % GROUP == 0 and (1 << HEAD_SHIFT) == HEAD_DIM
    assert SECTION_W == D_CONV == D_ATT and w_in.shape[2] == 8 * SECTION_W
    assert dec_seq % 8 == 0 and dec_seq <= GROUP and dec_batch % SAMPLE_BATCH_PER_STEP == 0

    win = w_in[0].astype(BF16)
    wout = w_out[0].astype(BF16)
    wg = w_ple_gate[0].astype(BF16)
    wp = w_ple_proj[0].astype(BF16)
    nin = norm_in[0][None, :]
    ncv = norm_conv[0][None, :]
    nat = norm_att[0][None, :]
    npl = ple_norm[0][None, :]
    nfin = final_norm[None, :]
    cw = conv_w[0]
    gvec = _bias_vector(rel_bias[0])

    weight_specs = [
        _const_spec(win.shape), _const_spec(wout.shape), _const_spec(wg.shape), _const_spec(wp.shape),
        _const_spec(nin.shape), _const_spec(cw.shape), _const_spec(ncv.shape), _const_spec(nat.shape),
        _const_spec(npl.shape), _const_spec(nfin.shape),
    ]
    weights = (win, wout, wg, wp, nin, cw, ncv, nat, npl, nfin)

    n_t = seq // PROMPT_TILE
    row_spec = lambda w: pl.BlockSpec((None, PROMPT_TILE, w), lambda b, t: (b, t, 0))
    state_spec = pl.BlockSpec((None, D_ATT, HIST), lambda b, t: (b, 0, 0))
    y_p, k_p, v_p, c_p = pl.pallas_call(
        _prompt_kernel,
        grid=(batch, n_t),
        in_specs=[row_spec(D_MODEL), row_spec(D_PLE)] + weight_specs + [_const_spec(gvec.shape)],
        out_specs=[row_spec(D_MODEL), state_spec, state_spec,
                   pl.BlockSpec((None, 8, D_CONV), lambda b, t: (b, 0, 0))],
        out_shape=[jax.ShapeDtypeStruct((batch, seq, D_MODEL), F32),
                   jax.ShapeDtypeStruct((batch, D_ATT, HIST), F32),
                   jax.ShapeDtypeStruct((batch, D_ATT, HIST), F32),
                   jax.ShapeDtypeStruct((batch, 8, D_CONV), F32)],
        scratch_shapes=[pltpu.VMEM((HIST + PROMPT_TILE, D_ATT), BF16),
                        pltpu.VMEM((HIST + PROMPT_TILE, D_ATT), BF16),
                        pltpu.VMEM((8, D_CONV), F32),
                        pltpu.VMEM((8, D_CONV), F32),
                        pltpu.VMEM((N_SLABS, KBLOCKS, SLAB_HEADS * GROUP, LANES), F32)],
        compiler_params=pltpu.CompilerParams(dimension_semantics=("arbitrary", "arbitrary"),
                                             vmem_limit_bytes=VMEM_LIMIT_BYTES),
        name="prompt_layer",
    )(x_prompt, p_prompt[0], *weights, gvec)

    nb = SAMPLE_BATCH_PER_STEP
    rows = nb * dec_seq
    flat_spec = lambda w: pl.BlockSpec((rows, w), lambda i: (i, 0))
    cache_spec = pl.BlockSpec((nb, N_HEADS, HEAD_DIM, HIST), lambda i: (i, 0, 0, 0))
    y_s, k_s, v_s, c_s = pl.pallas_call(
        _sample_kernel,
        grid=(dec_batch // nb,),
        in_specs=[flat_spec(D_MODEL), flat_spec(D_PLE), cache_spec, cache_spec,
                  pl.BlockSpec((nb, CONV_W - 1, D_CONV), lambda i: (i, 0, 0))]
                 + weight_specs + [_const_spec(gvec.shape)],
        out_specs=[flat_spec(D_MODEL), flat_spec(D_ATT), flat_spec(D_ATT),
                   pl.BlockSpec((nb, 8, D_CONV), lambda i: (i, 0, 0))],
        out_shape=[jax.ShapeDtypeStruct((dec_batch * dec_seq, D_MODEL), F32),
                   jax.ShapeDtypeStruct((dec_batch * dec_seq, D_ATT), F32),
                   jax.ShapeDtypeStruct((dec_batch * dec_seq, D_ATT), F32),
                   jax.ShapeDtypeStruct((dec_batch, 8, D_CONV), F32)],
        scratch_shapes=[pltpu.VMEM((nb, GROUP, D_ATT), BF16), pltpu.VMEM((nb, GROUP, D_ATT), BF16),
                        pltpu.VMEM((N_SLABS, SLAB_HEADS * dec_seq, KWIN), F32)],
        compiler_params=pltpu.CompilerParams(dimension_semantics=("arbitrary",),
                                             vmem_limit_bytes=VMEM_LIMIT_BYTES),
        name="sample_layer",
    )(x_sample.reshape(dec_batch * dec_seq, D_MODEL), p_sample[0].reshape(dec_batch * dec_seq, D_PLE),
      jnp.transpose(cache_k[0], (0, 2, 3, 1)), jnp.transpose(cache_v[0], (0, 2, 3, 1)),
      state_conv[0], *weights, gvec)

    key_major = lambda s: jnp.transpose(s.reshape(batch, N_HEADS, HEAD_DIM, HIST), (0, 3, 1, 2))[None]
    return (y_p,
            y_s.reshape(dec_batch, dec_seq, D_MODEL),
            key_major(k_p),
            key_major(v_p),
            c_p[None, :, 8 - (CONV_W - 1):, :],
            k_s.reshape(1, dec_batch, dec_seq, N_HEADS, HEAD_DIM),
            v_s.reshape(1, dec_batch, dec_seq, N_HEADS, HEAD_DIM),
            c_s[None, :, 8 - (CONV_W - 1):, :])
```

```python
import jax
import jax.numpy as jnp
from jax import lax
from jax.experimental import pallas as pl
from jax.experimental.pallas import tpu as pltpu

D_MODEL = 1024
D_CONV = 512
D_ATT = 512
HEAD_DIM = 64
HEAD_SHIFT = 6
N_HEADS = 8
SLAB_HEADS = 4
N_SLABS = N_HEADS // SLAB_HEADS
LANES = 128
SLAB_W = SLAB_HEADS * HEAD_DIM
CHUNK = 64
N_PAST_CHUNKS = 8
HIST = N_PAST_CHUNKS * CHUNK
REL_CLIP = 128
D_PLE = 256
CONV_W = 3
EPS = 1e-6
NEG = -1e30
SCALE = HEAD_DIM ** -0.5
SECTION_W = 512
SEC_H, SEC_B, SEC_C, SEC_ZC, SEC_Q, SEC_K, SEC_V, SEC_ZA = range(8)

GROUP = 2 * CHUNK
KWIN = HIST + GROUP
KBLOCKS = KWIN // LANES
FLAT = KWIN - REL_CLIP
PROMPT_TILE = 512
UNIT = 512
SAMPLE_BATCH_PER_STEP = 4
VMEM_LIMIT_BYTES = 56 * 1024 * 1024

F32 = jnp.float32
BF16 = jnp.bfloat16


def _rms(x, g):
    ms = jnp.mean(x * x, axis=-1, keepdims=True)
    return x * lax.rsqrt(ms + EPS) * g


def _sigmoid(x):
    return 1.0 / (1.0 + jnp.exp(-x))


def _dot(a, b):
    return jnp.dot(a, b, preferred_element_type=F32)


def _run(steps):
    try:
        while True:
            next(steps)
    except StopIteration as done:
        return done.value


def _chain(*stages):
    results = []
    for steps in stages:
        results.append((yield from steps))
    return results


def _zip(main, side, side_per_main):
    side_val, side_done = None, False
    while True:
        try:
            next(main)
        except StopIteration as done:
            main_val = done.value
            break
        for _ in range(side_per_main):
            if not side_done:
                try:
                    next(side)
                except StopIteration as done:
                    side_val, side_done = done.value, True
    if not side_done:
        side_val = _run(side)
    return main_val, side_val


def _proj(xn, win_ref, section):
    return _dot(xn, win_ref[:, SECTION_W * section:SECTION_W * (section + 1)])


def _qkv(x, win_ref, nin_ref):
    xn = _rms(x, nin_ref[...]).astype(BF16)
    q = (_proj(xn, win_ref, SEC_Q) * SCALE).astype(BF16)
    yield
    k = _proj(xn, win_ref, SEC_K)
    yield
    v = _proj(xn, win_ref, SEC_V)
    return xn, q, k, v


def _conv_gate(xn, win_ref, cw_ref, ncv_ref, prev_rows):
    h = _proj(xn, win_ref, SEC_H)
    yield
    u = _proj(xn, win_ref, SEC_C) * h
    u1, u2 = prev_rows(u)
    cv = cw_ref[0:1, :] * u2 + cw_ref[1:2, :] * u1 + cw_ref[2:3, :] * u
    yield
    zc = _proj(xn, win_ref, SEC_ZC)
    gz = zc * _sigmoid(zc)
    yield
    yc = _proj(xn, win_ref, SEC_B) * cv * gz
    yc_n = _rms(yc, ncv_ref[...])
    yield
    za = _proj(xn, win_ref, SEC_ZA)
    return u, yc_n, za


def _dot_t(a, b):
    return lax.dot_general(a, b, (((1,), (1,)), ((), ())), preferred_element_type=F32)


def _attend(members):
    r = members[0][0].shape[0]
    lane = lax.broadcasted_iota(jnp.int32, members[0][0].shape, 1) >> HEAD_SHIFT
    scores = []
    for qslab, keys, _, bias in members:
        zero = jnp.zeros_like(qslab)
        qs = jnp.concatenate([jnp.where(lane == j, qslab, zero) for j in range(SLAB_HEADS)], axis=0)
        parts = [_dot(qs, k) if transposed else _dot_t(qs, k) for k, transposed in keys]
        s = parts[0] if len(parts) == 1 else jnp.concatenate(parts, axis=1)
        scores.append(s + bias)
    s = scores[0] if len(scores) == 1 else jnp.concatenate(scores, axis=0)
    m = jnp.max(s, axis=-1, keepdims=True)
    e = jnp.exp(s - m)
    inv_l = 1.0 / jnp.sum(e, axis=-1, keepdims=True)
    p = e.astype(BF16)
    outs = []
    for i, (_, _, values, _) in enumerate(members):
        rows = slice(i * SLAB_HEADS * r, (i + 1) * SLAB_HEADS * r)
        o, col = None, 0
        for v, transposed in values:
            n = v.shape[1] if transposed else v.shape[0]
            pv = _dot_t(p[rows, col:col + n], v) if transposed else _dot(p[rows, col:col + n], v)
            o = pv if o is None else o + pv
            col += n
        o = o * inv_l[rows]
        out = o[(SLAB_HEADS - 1) * r:]
        for j in reversed(range(SLAB_HEADS - 1)):
            out = jnp.where(lane <= j, o[j * r:(j + 1) * r], out)
        outs.append(out)
    return outs


def _build_bias(gvec_ref, n_rows, in_band, store):
    var0 = FLAT - GROUP
    width = KWIN - var0 + GROUP
    i = lax.broadcasted_iota(jnp.int32, (n_rows, width), 0)
    band = in_band(lax.broadcasted_iota(jnp.int32, (n_rows, KWIN), 0),
                   lax.broadcasted_iota(jnp.int32, (n_rows, KWIN), 1))
    for h in range(N_HEADS):
        x = jnp.broadcast_to(gvec_ref[h:h + 1, var0:], (n_rows, width))
        bit = 0
        while (1 << bit) < n_rows:
            x = jnp.where(((i >> bit) & 1) == 1, pltpu.roll(x, 1 << bit, 1), x)
            bit += 1
        flat = jnp.broadcast_to(gvec_ref[h:h + 1, 0:var0], (n_rows, var0))
        store(h, jnp.where(band, jnp.concatenate([flat, x[:, GROUP:]], axis=1), NEG))


def _tail(x, yc_n, ya, za, p, wout_ref, wg_ref, wp_ref, nat_ref, npl_ref, nfin_ref):
    pe = _dot(p.astype(BF16), wp_ref[...])
    ya_n = _rms(ya * (za * _sigmoid(za)), nat_ref[...])
    yield
    y = _dot(yc_n.astype(BF16), wout_ref[0:D_CONV, :]) + _dot(ya_n.astype(BF16), wout_ref[D_CONV:, :])
    x1 = x + y
    yield
    gate = _sigmoid(_dot(_rms(x1, npl_ref[...]).astype(BF16), wg_ref[...]))
    x2 = x1 + gate * pe
    return _rms(x2, nfin_ref[...])


def _prompt_kernel(x_ref, p_ref, win_ref, wout_ref, wg_ref, wp_ref, nin_ref, cw_ref, ncv_ref,
                   nat_ref, npl_ref, nfin_ref, gvec_ref,
                   y_ref, ks_ref, vs_ref, cs_ref, kbuf, vbuf, ucar, unext, bias_ref):
    t = pl.program_id(1)
    tile = x_ref.shape[0]

    @pl.when((pl.program_id(0) == 0) & (t == 0))
    def _():
        def store(h, tb):
            j = h % SLAB_HEADS
            for cb in range(KBLOCKS):
                bias_ref[h // SLAB_HEADS, cb, j * GROUP:(j + 1) * GROUP, :] = tb[:, cb * LANES:(cb + 1) * LANES]

        _build_bias(gvec_ref, GROUP,
                    lambda i, s: ((i < CHUNK) & (s < HIST + CHUNK)) | ((i >= CHUNK) & (s >= CHUNK)), store)

    @pl.when(t > 0)
    def _():
        kbuf[0:HIST, :] = kbuf[tile:, :]
        vbuf[0:HIST, :] = vbuf[tile:, :]
        ucar[...] = unext[...]

    n_units = tile // UNIT
    row = lax.broadcasted_iota(jnp.int32, (UNIT, D_CONV), 0)

    def qkv(h):
        rows = slice(h * UNIT, (h + 1) * UNIT)
        xn, q, k, v = yield from _qkv(x_ref[rows, :], win_ref, nin_ref)
        kbuf[HIST + h * UNIT:HIST + (h + 1) * UNIT, :] = k.astype(BF16)
        vbuf[HIST + h * UNIT:HIST + (h + 1) * UNIT, :] = v.astype(BF16)
        ks_ref[:, rows] = k.T
        vs_ref[:, rows] = v.T
        return xn, q

    def conv_gate(xn, c1, c2):
        def prev_rows(u):
            u1 = jnp.where(row == 0, c1, pltpu.roll(u, 1, 0))
            u2 = jnp.where(row == 0, c2, jnp.where(row == 1, c1, pltpu.roll(u, 2, 0)))
            return u1, u2

        return _conv_gate(xn, win_ref, cw_ref, ncv_ref, prev_rows)

    def attention(h, q, first_tile):
        groups = []
        for gl in range(UNIT // GROUP):
            g = h * (UNIT // GROUP) + gl
            cb0 = max(0, (HIST - g * GROUP) // LANES) if first_tile else 0
            key_rows = slice(g * GROUP + cb0 * LANES, g * GROUP + KWIN)
            slabs = []
            for p in range(N_SLABS):
                lanes = slice(p * SLAB_W, (p + 1) * SLAB_W)
                bias = jnp.concatenate([bias_ref[p, cb] for cb in range(cb0, KBLOCKS)], axis=1)
                slabs.extend(_attend([(q[gl * GROUP:(gl + 1) * GROUP, lanes],
                                       [(kbuf[key_rows, lanes], False)], [(vbuf[key_rows, lanes], False)], bias)]))
                yield
            groups.append(jnp.concatenate(slabs, axis=1))
        return jnp.concatenate(groups, axis=0)

    def tail(h, yc_n, ya, za):
        rows = slice(h * UNIT, (h + 1) * UNIT)
        y_ref[rows, :] = yield from _tail(x_ref[rows, :], yc_n, ya, za, p_ref[rows, :],
                                          wout_ref, wg_ref, wp_ref, nat_ref, npl_ref, nfin_ref)

    def layer(first_tile):
        xn, q = _run(qkv(0))
        if first_tile:
            c1 = c2 = jnp.zeros((1, D_CONV), F32)
        else:
            c1, c2 = ucar[7:8, :], ucar[6:7, :]
        done = None
        for h in range(n_units):
            side = [conv_gate(xn, c1, c2)]
            if h + 1 < n_units:
                side.append(qkv(h + 1))
            if done is not None:
                side.append(tail(h - 1, *done))
            ya, results = _zip(attention(h, q, first_tile), _chain(*side), 5)
            u, yc_n, za = results[0]
            done = (yc_n, ya, za)
            c1, c2 = u[UNIT - 1:UNIT, :], u[UNIT - 2:UNIT - 1, :]
            if h + 1 < n_units:
                xn, q = results[1]
        _run(tail(n_units - 1, *done))
        unext[...] = u[UNIT - 8:, :]
        cs_ref[...] = u[UNIT - 8:, :]

    pl.when(t == 0)(lambda: layer(True))
    pl.when(t > 0)(lambda: layer(False))


def _sample_kernel(x_ref, p_ref, ck_ref, cv_ref, sc_ref, win_ref, wout_ref, wg_ref, wp_ref, nin_ref,
                   cw_ref, ncv_ref, nat_ref, npl_ref, nfin_ref, gvec_ref,
                   y_ref, ks_ref, vs_ref, cs_ref, knew, vnew, bias_ref):
    nb = ck_ref.shape[0]
    rows = x_ref.shape[0]
    seq = rows // nb

    @pl.when(pl.program_id(0) == 0)
    def _():
        def store(h, tb):
            j = h % SLAB_HEADS
            bias_ref[h // SLAB_HEADS, j * seq:(j + 1) * seq, :] = tb

        _build_bias(gvec_ref, seq, lambda i, s: s < HIST + seq, store)
        knew[:, seq:, :] = jnp.zeros((nb, GROUP - seq, D_ATT), BF16)
        vnew[:, seq:, :] = jnp.zeros((nb, GROUP - seq, D_ATT), BF16)

    x = x_ref[...]
    rmod = lax.broadcasted_iota(jnp.int32, (rows, D_CONV), 0) % seq

    def prev_rows(u):
        e1 = jnp.concatenate([jnp.broadcast_to(sc_ref[j, 1:2, :], (seq, D_CONV)) for j in range(nb)], axis=0)
        e2 = jnp.concatenate([jnp.broadcast_to(sc_ref[j, 0:1, :], (seq, D_CONV)) for j in range(nb)], axis=0)
        u1 = jnp.where(rmod == 0, e1, pltpu.roll(u, 1, 0))
        u2 = jnp.where(rmod == 0, e2, jnp.where(rmod == 1, e1, pltpu.roll(u, 2, 0)))
        return u1, u2

    xn, q, k, v = _run(_qkv(x, win_ref, nin_ref))
    kt, vt = k.T, v.T
    for j in range(nb):
        ks_ref[j] = kt[:, j * seq:(j + 1) * seq]
        vs_ref[j] = vt[:, j * seq:(j + 1) * seq]
    kb = k.astype(BF16)
    vb = v.astype(BF16)

    def attention():
        cache_kt, cache_vt = [], []
        for j in range(nb):
            rs = slice(j * seq, (j + 1) * seq)
            knew[j, 0:seq, :] = kb[rs, :]
            vnew[j, 0:seq, :] = vb[rs, :]
            cache_kt.append(ck_ref[j].reshape(D_ATT, HIST).astype(BF16))
            cache_vt.append(cv_ref[j].reshape(D_ATT, HIST).astype(BF16))
        slabs = []
        for p in range(N_SLABS):
            lanes = slice(p * SLAB_W, (p + 1) * SLAB_W)
            outs = _attend([(q[j * seq:(j + 1) * seq, lanes],
                             [(cache_kt[j][lanes, :], True), (knew[j, :, lanes], False)],
                             [(cache_vt[j][lanes, :], True), (vnew[j, :, lanes], False)], bias_ref[p])
                            for j in range(nb)])
            slabs.append(jnp.concatenate(outs, axis=0))
            yield
        return jnp.concatenate(slabs, axis=1)

    ya, (u, yc_n, za) = _zip(attention(), _conv_gate(xn, win_ref, cw_ref, ncv_ref, prev_rows), 3)
    for j in range(nb):
        cs_ref[j] = u[(j + 1) * seq - 8:(j + 1) * seq, :]
    y_ref[...] = _run(_tail(x, yc_n, ya, za, p_ref[...], wout_ref, wg_ref, wp_ref, nat_ref, npl_ref, nfin_ref))


def _bias_vector(rel_bias):
    width = KWIN + GROUP
    assert width - FLAT == 2 * REL_CLIP
    return jnp.concatenate([jnp.broadcast_to(rel_bias[:, 2 * REL_CLIP:], (N_HEADS, FLAT)),
                            rel_bias[:, ::-1][:, :2 * REL_CLIP]], axis=1)


def _const_spec(shape):
    nd = len(shape)
    return pl.BlockSpec(shape, lambda *_: (0,) * nd, pipeline_mode=pl.Buffered(1))


def kernel(x_prompt, x_sample, cache_k, cache_v, state_conv, p_prompt, p_sample, norm_in, w_in, conv_w,
           rel_bias, norm_conv, norm_att, w_out, ple_norm, w_ple_gate, w_ple_proj, final_norm):
    depth = w_in.shape[0]
    assert depth == 1, "single-layer kernel"
    batch, seq, _ = x_prompt.shape
    dec_batch, dec_seq, _ = x_sample.shape
    kv_win = cache_k.shape[2]
    assert kv_win == HIST and PROMPT_TILE == HIST and seq % PROMPT_TILE == 0
    assert PROMPT_TILE % UNIT == 0 and UNIT % GROUP == 0 and (1 << HEAD_SHIFT) == HEAD_DIM
    assert SECTION_W == D_CONV == D_ATT and w_in.shape[2] == 8 * SECTION_W
    assert dec_seq % 8 == 0 and dec_seq <= GROUP and dec_batch % SAMPLE_BATCH_PER_STEP == 0
    assert (SAMPLE_BATCH_PER_STEP * dec_seq) % LANES == 0

    win = w_in[0].astype(BF16)
    wout = w_out[0].astype(BF16)
    wg = w_ple_gate[0].astype(BF16)
    wp = w_ple_proj[0].astype(BF16)
    nin = norm_in[0][None, :]
    ncv = norm_conv[0][None, :]
    nat = norm_att[0][None, :]
    npl = ple_norm[0][None, :]
    nfin = final_norm[None, :]
    cw = conv_w[0]
    gvec = _bias_vector(rel_bias[0])

    weight_specs = [
        _const_spec(win.shape), _const_spec(wout.shape), _const_spec(wg.shape), _const_spec(wp.shape),
        _const_spec(nin.shape), _const_spec(cw.shape), _const_spec(ncv.shape), _const_spec(nat.shape),
        _const_spec(npl.shape), _const_spec(nfin.shape),
    ]
    weights = (win, wout, wg, wp, nin, cw, ncv, nat, npl, nfin)

    n_t = seq // PROMPT_TILE
    row_spec = lambda w: pl.BlockSpec((None, PROMPT_TILE, w), lambda b, t: (b, t, 0))
    state_spec = pl.BlockSpec((None, D_ATT, HIST), lambda b, t: (b, 0, 0))
    y_p, k_p, v_p, c_p = pl.pallas_call(
        _prompt_kernel,
        grid=(batch, n_t),
        in_specs=[row_spec(D_MODEL), row_spec(D_PLE)] + weight_specs + [_const_spec(gvec.shape)],
        out_specs=[row_spec(D_MODEL), state_spec, state_spec,
                   pl.BlockSpec((None, 8, D_CONV), lambda b, t: (b, 0, 0))],
        out_shape=[jax.ShapeDtypeStruct((batch, seq, D_MODEL), F32),
                   jax.ShapeDtypeStruct((batch, D_ATT, HIST), F32),
                   jax.ShapeDtypeStruct((batch, D_ATT, HIST), F32),
                   jax.ShapeDtypeStruct((batch, 8, D_CONV), F32)],
        scratch_shapes=[pltpu.VMEM((HIST + PROMPT_TILE, D_ATT), BF16),
                        pltpu.VMEM((HIST + PROMPT_TILE, D_ATT), BF16),
                        pltpu.VMEM((8, D_CONV), F32),
                        pltpu.VMEM((8, D_CONV), F32),
                        pltpu.VMEM((N_SLABS, KBLOCKS, SLAB_HEADS * GROUP, LANES), F32)],
        compiler_params=pltpu.CompilerParams(dimension_semantics=("arbitrary", "arbitrary"),
                                             vmem_limit_bytes=VMEM_LIMIT_BYTES),
        name="prompt_layer",
    )(x_prompt, p_prompt[0], *weights, gvec)

    nb = SAMPLE_BATCH_PER_STEP
    rows = nb * dec_seq
    flat_spec = lambda w: pl.BlockSpec((rows, w), lambda i: (i, 0))
    cache_spec = pl.BlockSpec((nb, N_HEADS, HEAD_DIM, HIST), lambda i: (i, 0, 0, 0))
    new_rows_spec = pl.BlockSpec((nb, D_ATT, dec_seq), lambda i: (i, 0, 0))
    y_s, k_s, v_s, c_s = pl.pallas_call(
        _sample_kernel,
        grid=(dec_batch // nb,),
        in_specs=[flat_spec(D_MODEL), flat_spec(D_PLE), cache_spec, cache_spec,
                  pl.BlockSpec((nb, CONV_W - 1, D_CONV), lambda i: (i, 0, 0))]
                 + weight_specs + [_const_spec(gvec.shape)],
        out_specs=[flat_spec(D_MODEL), new_rows_spec, new_rows_spec,
                   pl.BlockSpec((nb, 8, D_CONV), lambda i: (i, 0, 0))],
        out_shape=[jax.ShapeDtypeStruct((dec_batch * dec_seq, D_MODEL), F32),
                   jax.ShapeDtypeStruct((dec_batch, D_ATT, dec_seq), F32),
                   jax.ShapeDtypeStruct((dec_batch, D_ATT, dec_seq), F32),
                   jax.ShapeDtypeStruct((dec_batch, 8, D_CONV), F32)],
        scratch_shapes=[pltpu.VMEM((nb, GROUP, D_ATT), BF16), pltpu.VMEM((nb, GROUP, D_ATT), BF16),
                        pltpu.VMEM((N_SLABS, SLAB_HEADS * dec_seq, KWIN), F32)],
        compiler_params=pltpu.CompilerParams(dimension_semantics=("arbitrary",),
                                             vmem_limit_bytes=VMEM_LIMIT_BYTES),
        name="sample_layer",
    )(x_sample.reshape(dec_batch * dec_seq, D_MODEL), p_sample[0].reshape(dec_batch * dec_seq, D_PLE),
      jnp.transpose(cache_k[0], (0, 2, 3, 1)), jnp.transpose(cache_v[0], (0, 2, 3, 1)),
      state_conv[0], *weights, gvec)

    key_major = lambda s: jnp.transpose(s.reshape(s.shape[0], N_HEADS, HEAD_DIM, s.shape[2]), (0, 3, 1, 2))[None]
    return (y_p,
            y_s.reshape(dec_batch, dec_seq, D_MODEL),
            key_major(k_p),
            key_major(v_p),
            c_p[None, :, 8 - (CONV_W - 1):, :],
            key_major(k_s),
            key_major(v_s),
            c_s[None, :, 8 - (CONV_W - 1):, :])
```

```python
import jax
import jax.numpy as jnp
from jax import lax
from jax.experimental import pallas as pl
from jax.experimental.pallas import tpu as pltpu

D_MODEL = 1024
D_CONV = 512
D_ATT = 512
HEAD_DIM = 64
HEAD_SHIFT = 6
N_HEADS = 8
SLAB_HEADS = 4
N_SLABS = N_HEADS // SLAB_HEADS
LANES = 128
SLAB_W = SLAB_HEADS * HEAD_DIM
CHUNK = 64
N_PAST_CHUNKS = 8
HIST = N_PAST_CHUNKS * CHUNK
REL_CLIP = 128
D_PLE = 256
CONV_W = 3
EPS = 1e-6
NEG = -1e30
SCALE = HEAD_DIM ** -0.5
SECTION_W = 512
SEC_H, SEC_B, SEC_C, SEC_ZC, SEC_Q, SEC_K, SEC_V, SEC_ZA = range(8)

GROUP = 2 * CHUNK
KWIN = HIST + GROUP
KBLOCKS = KWIN // LANES
FLAT = KWIN - REL_CLIP
PROMPT_TILE = 512
UNIT = 512
SAMPLE_BATCH_PER_STEP = 4
VMEM_LIMIT_BYTES = 56 * 1024 * 1024

F32 = jnp.float32
BF16 = jnp.bfloat16


def _rms(x, g):
    ms = jnp.mean(x * x, axis=-1, keepdims=True)
    return x * lax.rsqrt(ms + EPS) * g


def _sigmoid(x):
    return 1.0 / (1.0 + jnp.exp(-x))


def _dot(a, b):
    return jnp.dot(a, b, preferred_element_type=F32)


def _run(steps):
    try:
        while True:
            next(steps)
    except StopIteration as done:
        return done.value


def _chain(*stages):
    results = []
    for steps in stages:
        results.append((yield from steps))
    return results


def _zip(main, side, side_per_main):
    side_val, side_done = None, False
    while True:
        try:
            next(main)
        except StopIteration as done:
            main_val = done.value
            break
        for _ in range(side_per_main):
            if not side_done:
                try:
                    next(side)
                except StopIteration as done:
                    side_val, side_done = done.value, True
    if not side_done:
        side_val = _run(side)
    return main_val, side_val


def _proj(xn, win_ref, section):
    return _dot(xn, win_ref[:, SECTION_W * section:SECTION_W * (section + 1)])


def _qkv(x, win_ref, nin_ref):
    xn = _rms(x, nin_ref[...]).astype(BF16)
    q = (_proj(xn, win_ref, SEC_Q) * SCALE).astype(BF16)
    yield
    k = _proj(xn, win_ref, SEC_K)
    yield
    v = _proj(xn, win_ref, SEC_V)
    return xn, q, k, v


def _conv_gate(xn, win_ref, cw_ref, ncv_ref, prev_rows):
    h = _proj(xn, win_ref, SEC_H)
    yield
    u = _proj(xn, win_ref, SEC_C) * h
    u1, u2 = prev_rows(u)
    cv = cw_ref[0:1, :] * u2 + cw_ref[1:2, :] * u1 + cw_ref[2:3, :] * u
    yield
    zc = _proj(xn, win_ref, SEC_ZC)
    gz = zc * _sigmoid(zc)
    yield
    yc = _proj(xn, win_ref, SEC_B) * cv * gz
    yc_n = _rms(yc, ncv_ref[...])
    yield
    za = _proj(xn, win_ref, SEC_ZA)
    return u, yc_n, za


def _dot_t(a, b):
    return lax.dot_general(a, b, (((1,), (1,)), ((), ())), preferred_element_type=F32)


def _attend(members):
    r = members[0][0].shape[0]
    lane = lax.broadcasted_iota(jnp.int32, members[0][0].shape, 1) >> HEAD_SHIFT
    scores = []
    for qslab, keys, _, bias in members:
        zero = jnp.zeros_like(qslab)
        qs = jnp.concatenate([jnp.where(lane == j, qslab, zero) for j in range(SLAB_HEADS)], axis=0)
        parts = [_dot(qs, k) if transposed else _dot_t(qs, k) for k, transposed in keys]
        s = parts[0] if len(parts) == 1 else jnp.concatenate(parts, axis=1)
        scores.append(s + bias)
    s = scores[0] if len(scores) == 1 else jnp.concatenate(scores, axis=0)
    m = jnp.max(s, axis=-1, keepdims=True)
    e = jnp.exp(s - m)
    inv_l = 1.0 / jnp.sum(e, axis=-1, keepdims=True)
    p = e.astype(BF16)
    outs = []
    for i, (_, _, values, _) in enumerate(members):
        rows = slice(i * SLAB_HEADS * r, (i + 1) * SLAB_HEADS * r)
        o, col = None, 0
        for v, transposed in values:
            n = v.shape[1] if transposed else v.shape[0]
            pv = _dot_t(p[rows, col:col + n], v) if transposed else _dot(p[rows, col:col + n], v)
            o = pv if o is None else o + pv
            col += n
        o = o * inv_l[rows]
        out = o[(SLAB_HEADS - 1) * r:]
        for j in reversed(range(SLAB_HEADS - 1)):
            out = jnp.where(lane <= j, o[j * r:(j + 1) * r], out)
        outs.append(out)
    return outs


def _build_bias(gvec_ref, n_rows, in_band, store):
    var0 = FLAT - GROUP
    width = KWIN - var0 + GROUP
    i = lax.broadcasted_iota(jnp.int32, (n_rows, width), 0)
    band = in_band(lax.broadcasted_iota(jnp.int32, (n_rows, KWIN), 0),
                   lax.broadcasted_iota(jnp.int32, (n_rows, KWIN), 1))
    for h in range(N_HEADS):
        x = jnp.broadcast_to(gvec_ref[h:h + 1, var0:], (n_rows, width))
        bit = 0
        while (1 << bit) < n_rows:
            x = jnp.where(((i >> bit) & 1) == 1, pltpu.roll(x, 1 << bit, 1), x)
            bit += 1
        flat = jnp.broadcast_to(gvec_ref[h:h + 1, 0:var0], (n_rows, var0))
        store(h, jnp.where(band, jnp.concatenate([flat, x[:, GROUP:]], axis=1), NEG))


def _tail(x, yc_n, ya, za, p, wout_ref, wg_ref, wp_ref, nat_ref, npl_ref, nfin_ref):
    pe = _dot(p.astype(BF16), wp_ref[...])
    ya_n = _rms(ya * (za * _sigmoid(za)), nat_ref[...])
    yield
    y = _dot(yc_n.astype(BF16), wout_ref[0:D_CONV, :]) + _dot(ya_n.astype(BF16), wout_ref[D_CONV:, :])
    x1 = x + y
    yield
    gate = _sigmoid(_dot(_rms(x1, npl_ref[...]).astype(BF16), wg_ref[...]))
    x2 = x1 + gate * pe
    return _rms(x2, nfin_ref[...])


def _prompt_kernel(x_ref, p_ref, win_ref, wout_ref, wg_ref, wp_ref, nin_ref, cw_ref, ncv_ref,
                   nat_ref, npl_ref, nfin_ref, gvec_ref,
                   y_ref, ks_ref, vs_ref, cs_ref, kbuf, vbuf, ucar, unext, bias_ref):
    t = pl.program_id(1)
    tile = x_ref.shape[0]

    @pl.when((pl.program_id(0) == 0) & (t == 0))
    def _():
        def store(h, tb):
            j = h % SLAB_HEADS
            for cb in range(KBLOCKS):
                bias_ref[h // SLAB_HEADS, cb, j * GROUP:(j + 1) * GROUP, :] = tb[:, cb * LANES:(cb + 1) * LANES]

        _build_bias(gvec_ref, GROUP,
                    lambda i, s: ((i < CHUNK) & (s < HIST + CHUNK)) | ((i >= CHUNK) & (s >= CHUNK)), store)

    @pl.when(t > 0)
    def _():
        kbuf[0:HIST, :] = kbuf[tile:, :]
        vbuf[0:HIST, :] = vbuf[tile:, :]
        ucar[...] = unext[...]

    n_units = tile // UNIT
    row = lax.broadcasted_iota(jnp.int32, (UNIT, D_CONV), 0)

    def qkv(h):
        rows = slice(h * UNIT, (h + 1) * UNIT)
        xn, q, k, v = yield from _qkv(x_ref[rows, :], win_ref, nin_ref)
        kbuf[HIST + h * UNIT:HIST + (h + 1) * UNIT, :] = k.astype(BF16)
        vbuf[HIST + h * UNIT:HIST + (h + 1) * UNIT, :] = v.astype(BF16)
        ks_ref[:, rows] = k.T
        vs_ref[:, rows] = v.T
        return xn, q

    def conv_gate(xn, c1, c2):
        def prev_rows(u):
            u1 = jnp.where(row == 0, c1, pltpu.roll(u, 1, 0))
            u2 = jnp.where(row == 0, c2, jnp.where(row == 1, c1, pltpu.roll(u, 2, 0)))
            return u1, u2

        return _conv_gate(xn, win_ref, cw_ref, ncv_ref, prev_rows)

    def attention(h, q, first_tile):
        groups = []
        for gl in range(UNIT // GROUP):
            g = h * (UNIT // GROUP) + gl
            cb0 = max(0, (HIST - g * GROUP) // LANES) if first_tile else 0
            key_rows = slice(g * GROUP + cb0 * LANES, g * GROUP + KWIN)
            slabs = []
            for p in range(N_SLABS):
                lanes = slice(p * SLAB_W, (p + 1) * SLAB_W)
                bias = jnp.concatenate([bias_ref[p, cb] for cb in range(cb0, KBLOCKS)], axis=1)
                slabs.extend(_attend([(q[gl * GROUP:(gl + 1) * GROUP, lanes],
                                       [(kbuf[key_rows, lanes], False)], [(vbuf[key_rows, lanes], False)], bias)]))
                yield
            groups.append(jnp.concatenate(slabs, axis=1))
        return jnp.concatenate(groups, axis=0)

    def tail(h, yc_n, ya, za):
        rows = slice(h * UNIT, (h + 1) * UNIT)
        y_ref[rows, :] = yield from _tail(x_ref[rows, :], yc_n, ya, za, p_ref[rows, :],
                                          wout_ref, wg_ref, wp_ref, nat_ref, npl_ref, nfin_ref)

    def layer(first_tile):
        xn, q = _run(qkv(0))
        if first_tile:
            c1 = c2 = jnp.zeros((1, D_CONV), F32)
        else:
            c1, c2 = ucar[7:8, :], ucar[6:7, :]
        done = None
        for h in range(n_units):
            side = [conv_gate(xn, c1, c2)]
            if h + 1 < n_units:
                side.append(qkv(h + 1))
            if done is not None:
                side.append(tail(h - 1, *done))
            ya, results = _zip(attention(h, q, first_tile), _chain(*side), 5)
            u, yc_n, za = results[0]
            done = (yc_n, ya, za)
            c1, c2 = u[UNIT - 1:UNIT, :], u[UNIT - 2:UNIT - 1, :]
            if h + 1 < n_units:
                xn, q = results[1]
        _run(tail(n_units - 1, *done))
        unext[...] = u[UNIT - 8:, :]
        cs_ref[...] = u[UNIT - (CONV_W - 1):, :]

    pl.when(t == 0)(lambda: layer(True))
    pl.when(t > 0)(lambda: layer(False))


def _sample_kernel(x_ref, p_ref, ck_ref, cv_ref, sc_ref, win_ref, wout_ref, wg_ref, wp_ref, nin_ref,
                   cw_ref, ncv_ref, nat_ref, npl_ref, nfin_ref, gvec_ref,
                   y_ref, ks_ref, vs_ref, cs_ref, knew, vnew, bias_ref):
    nb = ck_ref.shape[0]
    rows = x_ref.shape[0]
    seq = rows // nb

    @pl.when(pl.program_id(0) == 0)
    def _():
        def store(h, tb):
            j = h % SLAB_HEADS
            bias_ref[h // SLAB_HEADS, j * seq:(j + 1) * seq, :] = tb

        _build_bias(gvec_ref, seq, lambda i, s: s < HIST + seq, store)
        knew[:, seq:, :] = jnp.zeros((nb, GROUP - seq, D_ATT), BF16)
        vnew[:, seq:, :] = jnp.zeros((nb, GROUP - seq, D_ATT), BF16)

    x = x_ref[...]
    rmod = lax.broadcasted_iota(jnp.int32, (rows, D_CONV), 0) % seq

    def prev_rows(u):
        e1 = jnp.concatenate([jnp.broadcast_to(sc_ref[j, 1:2, :], (seq, D_CONV)) for j in range(nb)], axis=0)
        e2 = jnp.concatenate([jnp.broadcast_to(sc_ref[j, 0:1, :], (seq, D_CONV)) for j in range(nb)], axis=0)
        u1 = jnp.where(rmod == 0, e1, pltpu.roll(u, 1, 0))
        u2 = jnp.where(rmod == 0, e2, jnp.where(rmod == 1, e1, pltpu.roll(u, 2, 0)))
        return u1, u2

    xn, q, k, v = _run(_qkv(x, win_ref, nin_ref))
    ks_ref[...] = k
    vs_ref[...] = v
    kb = k.astype(BF16)
    vb = v.astype(BF16)

    def attention():
        cache_kt, cache_vt = [], []
        for j in range(nb):
            rs = slice(j * seq, (j + 1) * seq)
            knew[j, 0:seq, :] = kb[rs, :]
            vnew[j, 0:seq, :] = vb[rs, :]
            cache_kt.append(ck_ref[j].reshape(D_ATT, HIST).astype(BF16))
            cache_vt.append(cv_ref[j].reshape(D_ATT, HIST).astype(BF16))
        slabs = []
        for p in range(N_SLABS):
            lanes = slice(p * SLAB_W, (p + 1) * SLAB_W)
            outs = _attend([(q[j * seq:(j + 1) * seq, lanes],
                             [(cache_kt[j][lanes, :], True), (knew[j, :, lanes], False)],
                             [(cache_vt[j][lanes, :], True), (vnew[j, :, lanes], False)], bias_ref[p])
                            for j in range(nb)])
            slabs.append(jnp.concatenate(outs, axis=0))
            yield
        return jnp.concatenate(slabs, axis=1)

    ya, (u, yc_n, za) = _zip(attention(), _conv_gate(xn, win_ref, cw_ref, ncv_ref, prev_rows), 3)
    for j in range(nb):
        cs_ref[j] = u[(j + 1) * seq - (CONV_W - 1):(j + 1) * seq, :]
    y_ref[...] = _run(_tail(x, yc_n, ya, za, p_ref[...], wout_ref, wg_ref, wp_ref, nat_ref, npl_ref, nfin_ref))


def _bias_vector(rel_bias):
    width = KWIN + GROUP
    assert width - FLAT == 2 * REL_CLIP
    return jnp.concatenate([jnp.broadcast_to(rel_bias[:, 2 * REL_CLIP:], (N_HEADS, FLAT)),
                            rel_bias[:, ::-1][:, :2 * REL_CLIP]], axis=1)


def _const_spec(shape):
    nd = len(shape)
    return pl.BlockSpec(shape, lambda *_: (0,) * nd, pipeline_mode=pl.Buffered(1))


def kernel(x_prompt, x_sample, cache_k, cache_v, state_conv, p_prompt, p_sample, norm_in, w_in, conv_w,
           rel_bias, norm_conv, norm_att, w_out, ple_norm, w_ple_gate, w_ple_proj, final_norm):
    depth = w_in.shape[0]
    assert depth == 1, "single-layer kernel"
    batch, seq, _ = x_prompt.shape
    dec_batch, dec_seq, _ = x_sample.shape
    kv_win = cache_k.shape[2]
    assert kv_win == HIST and PROMPT_TILE == HIST and seq % PROMPT_TILE == 0
    assert PROMPT_TILE % UNIT == 0 and UNIT % GROUP == 0 and (1 << HEAD_SHIFT) == HEAD_DIM
    assert SECTION_W == D_CONV == D_ATT and w_in.shape[2] == 8 * SECTION_W
    assert dec_seq % 8 == 0 and dec_seq <= GROUP and dec_batch % SAMPLE_BATCH_PER_STEP == 0

    win = w_in[0].astype(BF16)
    wout = w_out[0].astype(BF16)
    wg = w_ple_gate[0].astype(BF16)
    wp = w_ple_proj[0].astype(BF16)
    nin = norm_in[0][None, :]
    ncv = norm_conv[0][None, :]
    nat = norm_att[0][None, :]
    npl = ple_norm[0][None, :]
    nfin = final_norm[None, :]
    cw = conv_w[0]
    gvec = _bias_vector(rel_bias[0])

    weight_specs = [
        _const_spec(win.shape), _const_spec(wout.shape), _const_spec(wg.shape), _const_spec(wp.shape),
        _const_spec(nin.shape), _const_spec(cw.shape), _const_spec(ncv.shape), _const_spec(nat.shape),
        _const_spec(npl.shape), _const_spec(nfin.shape),
    ]
    weights = (win, wout, wg, wp, nin, cw, ncv, nat, npl, nfin)

    n_t = seq // PROMPT_TILE
    row_spec = lambda w: pl.BlockSpec((None, PROMPT_TILE, w), lambda b, t: (b, t, 0))
    state_spec = pl.BlockSpec((None, D_ATT, HIST), lambda b, t: (b, 0, 0))
    y_p, k_p, v_p, c_p = pl.pallas_call(
        _prompt_kernel,
        grid=(batch, n_t),
        in_specs=[row_spec(D_MODEL), row_spec(D_PLE)] + weight_specs + [_const_spec(gvec.shape)],
        out_specs=[row_spec(D_MODEL), state_spec, state_spec,
                   pl.BlockSpec((None, CONV_W - 1, D_CONV), lambda b, t: (b, 0, 0))],
        out_shape=[jax.ShapeDtypeStruct((batch, seq, D_MODEL), F32),
                   jax.ShapeDtypeStruct((batch, D_ATT, HIST), F32),
                   jax.ShapeDtypeStruct((batch, D_ATT, HIST), F32),
                   jax.ShapeDtypeStruct((batch, CONV_W - 1, D_CONV), F32)],
        scratch_shapes=[pltpu.VMEM((HIST + PROMPT_TILE, D_ATT), BF16),
                        pltpu.VMEM((HIST + PROMPT_TILE, D_ATT), BF16),
                        pltpu.VMEM((8, D_CONV), F32),
                        pltpu.VMEM((8, D_CONV), F32),
                        pltpu.VMEM((N_SLABS, KBLOCKS, SLAB_HEADS * GROUP, LANES), F32)],
        compiler_params=pltpu.CompilerParams(dimension_semantics=("arbitrary", "arbitrary"),
                                             vmem_limit_bytes=VMEM_LIMIT_BYTES),
        name="prompt_layer",
    )(x_prompt, p_prompt[0], *weights, gvec)

    nb = SAMPLE_BATCH_PER_STEP
    rows = nb * dec_seq
    flat_spec = lambda w: pl.BlockSpec((rows, w), lambda i: (i, 0))
    cache_spec = pl.BlockSpec((nb, N_HEADS, HEAD_DIM, HIST), lambda i: (i, 0, 0, 0))
    y_s, k_s, v_s, c_s = pl.pallas_call(
        _sample_kernel,
        grid=(dec_batch // nb,),
        in_specs=[flat_spec(D_MODEL), flat_spec(D_PLE), cache_spec, cache_spec,
                  pl.BlockSpec((nb, CONV_W - 1, D_CONV), lambda i: (i, 0, 0))]
                 + weight_specs + [_const_spec(gvec.shape)],
        out_specs=[flat_spec(D_MODEL), flat_spec(D_ATT), flat_spec(D_ATT),
                   pl.BlockSpec((nb, CONV_W - 1, D_CONV), lambda i: (i, 0, 0))],
        out_shape=[jax.ShapeDtypeStruct((dec_batch * dec_seq, D_MODEL), F32),
                   jax.ShapeDtypeStruct((dec_batch * dec_seq, D_ATT), F32),
                   jax.ShapeDtypeStruct((dec_batch * dec_seq, D_ATT), F32),
                   jax.ShapeDtypeStruct((dec_batch, CONV_W - 1, D_CONV), F32)],
        scratch_shapes=[pltpu.VMEM((nb, GROUP, D_ATT), BF16), pltpu.VMEM((nb, GROUP, D_ATT), BF16),
                        pltpu.VMEM((N_SLABS, SLAB_HEADS * dec_seq, KWIN), F32)],
        compiler_params=pltpu.CompilerParams(dimension_semantics=("arbitrary",),
                                             vmem_limit_bytes=VMEM_LIMIT_BYTES),
        name="sample_layer",
    )(x_sample.reshape(dec_batch * dec_seq, D_MODEL), p_sample[0].reshape(dec_batch * dec_seq, D_PLE),
      jnp.transpose(cache_k[0], (0, 2, 3, 1)), jnp.transpose(cache_v[0], (0, 2, 3, 1)),
      state_conv[0], *weights, gvec)

    key_major = lambda s: jnp.transpose(s.reshape(batch, N_HEADS, HEAD_DIM, HIST), (0, 3, 1, 2))[None]
    return (y_p,
            y_s.reshape(dec_batch, dec_seq, D_MODEL),
            key_major(k_p),
            key_major(v_p),
            c_p[None],
            k_s.reshape(1, dec_batch, dec_seq, N_HEADS, HEAD_DIM),
            v_s.reshape(1, dec_batch, dec_seq, N_HEADS, HEAD_DIM),
            c_s[None])
```
